```python
import math
import jax, jax.numpy as jnp
from jax import lax
import numpy as np

D_MODEL = 1024
BATCH = 8
SEQ = 2048
DEPTH = 4
DEC_BATCH = 128
DEC_SEQ = 8
PAST_LEN = 2048
PAGE_SIZE = 128

N_A = DEPTH // 2
N_B = DEPTH - N_A
N_HEADS = 8
HEAD_DIM = 64
V_DIM = 2 * HEAD_DIM
QK_WIDTH = 2 * N_HEADS * HEAD_DIM
V_WIDTH = N_HEADS * V_DIM
CONV_W = 31
_FF_RAW = -(-8 * D_MODEL // 3)
D_FF = -(-_FF_RAW // 256) * 256
N_BUCKETS = 32
MAX_DISTANCE = 128
Q_BLOCK = 128
EPS = 1e-6

kernel_name = "yoco_conformer_diffattn_decoder_step"


def rms_norm(x, g):
    xf = x.astype(jnp.float32)
    y = xf * lax.rsqrt(jnp.mean(xf * xf, axis=-1, keepdims=True) + EPS)
    return (y * g.astype(jnp.float32)).astype(x.dtype)


def lambda_init(layer_idx):
    return 0.8 - 0.6 * math.exp(-0.3 * layer_idx)


def rel_buckets(n):
    max_exact = N_BUCKETS // 2
    nf = jnp.maximum(n, 1).astype(jnp.float32)
    large = max_exact + (jnp.log(nf / max_exact) / math.log(MAX_DISTANCE / max_exact)
                         * (N_BUCKETS - max_exact)).astype(jnp.int32)
    large = jnp.minimum(large, N_BUCKETS - 1)
    return jnp.where(n < max_exact, n, large)


def diff_weights(scores, q_pos, k_pos, rel_bias, lam):
    dist = q_pos[:, None] - k_pos[None, :]
    bias = jnp.transpose(rel_bias[rel_buckets(jnp.maximum(dist, 0))], (2, 0, 1)).astype(jnp.float32)
    logits = scores + bias[None, :, None]
    logits = jnp.where((dist >= 0)[None, None, None], logits, -jnp.inf)
    p = jax.nn.softmax(logits, axis=-1)
    return p[:, :, 0] - lam * p[:, :, 1]


def prompt_diff_attn(q, k, v, rel_bias, lam):
    B, S = q.shape[0], q.shape[1]
    nqb = S // Q_BLOCK
    qb = q.reshape(B, nqb, Q_BLOCK, N_HEADS, 2, HEAD_DIM).transpose(1, 0, 2, 3, 4, 5)
    k_pos = jnp.arange(S)
    scale = HEAD_DIM ** -0.5

    def block(args):
        i, qi = args
        s = jnp.einsum('bqhcd,bkhcd->bhcqk', qi, k, preferred_element_type=jnp.float32) * scale
        q_pos = i * Q_BLOCK + jnp.arange(Q_BLOCK)
        a = diff_weights(s, q_pos, k_pos, rel_bias, lam)
        return jnp.einsum('bhqk,bkhe->bqhe', a.astype(v.dtype), v)

    out = lax.map(block, (jnp.arange(nqb), qb))
    return out.transpose(1, 0, 2, 3, 4).reshape(B, S, N_HEADS, V_DIM)


def sample_diff_attn(q, k_past, v_past, k_new, v_new, rel_bias, lam):
    P, T = k_past.shape[1], q.shape[1]
    scale = HEAD_DIM ** -0.5
    s_past = jnp.einsum('bqhcd,bkhcd->bhcqk', q, k_past, preferred_element_type=jnp.float32)
    s_new = jnp.einsum('bqhcd,bkhcd->bhcqk', q, k_new, preferred_element_type=jnp.float32)
    s = jnp.concatenate([s_past, s_new], axis=-1) * scale
    q_pos = P + jnp.arange(T)
    k_pos = jnp.arange(P + T)
    a = diff_weights(s, q_pos, k_pos, rel_bias, lam).astype(v_new.dtype)
    return (jnp.einsum('bhqk,bkhe->bqhe', a[..., :P], v_past)
            + jnp.einsum('bhqk,bkhe->bqhe', a[..., P:], v_new))


def conv_module(h, past, g_in, w1, b1, wdw, bdw, g_mid, w2, b2):
    a = rms_norm(h, g_in) @ w1 + b1
    glu = a[..., :D_MODEL] * jax.nn.sigmoid(a[..., D_MODEL:])
    u = jnp.concatenate([past.astype(glu.dtype), glu], axis=1)
    c = lax.conv_general_dilated(u, wdw[:, None, :].astype(u.dtype), window_strides=(1,), padding='VALID',
                                 dimension_numbers=('NWC', 'WIO', 'NWC'),
                                 feature_group_count=D_MODEL) + bdw
    c = jax.nn.silu(rms_norm(c, g_mid))
    return c @ w2 + b2, u[:, -(CONV_W - 1):]


def shared_kv(h, g, w_kv, k_g):
    B, T = h.shape[0], h.shape[1]
    kv = rms_norm(h, g) @ w_kv
    k = rms_norm(kv[..., :QK_WIDTH].reshape(B, T, N_HEADS, 2, HEAD_DIM), k_g)
    v = kv[..., QK_WIDTH:].reshape(B, T, N_HEADS, V_DIM)
    return k, v


def queries(h, g, w, q_g):
    B, T = h.shape[0], h.shape[1]
    q = (rms_norm(h, g) @ w).reshape(B, T, N_HEADS, 2, HEAD_DIM)
    return rms_norm(q, q_g)


def attn_out(o, g, w, lam_init):
    B, T = o.shape[0], o.shape[1]
    o = rms_norm(o, g) * (1.0 - lam_init)
    return o.reshape(B, T, V_WIDTH) @ w


def swiglu(h, g, wg, wu, wd):
    xn = rms_norm(h, g)
    return (jax.nn.silu(xn @ wg) * (xn @ wu)) @ wd


def setup_inputs(seed: int = 0) -> dict:
    key = jax.random.key(seed)
    ks = iter(jax.random.split(key, 48))

    def nrm(shape, scale):
        return jax.random.normal(next(ks), shape, jnp.float32) * scale

    def gain(shape):
        return 1.0 + nrm(shape, 0.02)

    n_pages = PAST_LEN // PAGE_SIZE
    n_used = DEC_BATCH * n_pages
    n_pool = n_used + max(1, n_used // 4)
    perm = jax.random.permutation(next(ks), n_pool)
    page_table = perm[:n_used].reshape(DEC_BATCH, n_pages).astype(jnp.int32)
    d_in = D_MODEL ** -0.5
    return {
        'x_prompt': nrm((BATCH, SEQ, D_MODEL), 1.0),
        'x_sample': nrm((DEC_BATCH, DEC_SEQ, D_MODEL), 1.0),
        'state_conv': nrm((N_A, DEC_BATCH, CONV_W - 1, D_MODEL), 0.5),
        'cache_k': nrm((n_pool, PAGE_SIZE, N_HEADS, 2, HEAD_DIM), 1.0),
        'cache_v': nrm((n_pool, PAGE_SIZE, N_HEADS, V_DIM), 1.0),
        'page_table': page_table,
        'rel_bias': nrm((N_BUCKETS, N_HEADS), 0.5),
        'conv_norm': gain((N_A, D_MODEL)),
        'w_pw1': nrm((N_A, D_MODEL, 2 * D_MODEL), d_in),
        'b_pw1': nrm((N_A, 2 * D_MODEL), 0.02),
        'w_dw': nrm((N_A, CONV_W, D_MODEL), CONV_W ** -0.5),
        'b_dw': nrm((N_A, D_MODEL), 0.02),
        'conv_mid_norm': gain((N_A, D_MODEL)),
        'w_pw2': nrm((N_A, D_MODEL, D_MODEL), d_in),
        'b_pw2': nrm((N_A, D_MODEL), 0.02),
        'kv_norm': gain((D_MODEL,)),
        'w_kv': nrm((D_MODEL, QK_WIDTH + V_WIDTH), d_in),
        'k_norm': gain((HEAD_DIM,)),
        'attn_norm': gain((N_B, D_MODEL)),
        'w_q': nrm((N_B, D_MODEL, QK_WIDTH), d_in),
        'q_norm': gain((N_B, HEAD_DIM)),
        'lambda_q1': nrm((N_B, HEAD_DIM), 0.1),
        'lambda_k1': nrm((N_B, HEAD_DIM), 0.1),
        'lambda_q2': nrm((N_B, HEAD_DIM), 0.1),
        'lambda_k2': nrm((N_B, HEAD_DIM), 0.1),
        'sub_norm': gain((N_B, V_DIM)),
        'w_o': nrm((N_B, V_WIDTH, D_MODEL), V_WIDTH ** -0.5),
        'ffn_norm': gain((DEPTH, D_MODEL)),
        'w_gate': nrm((DEPTH, D_MODEL, D_FF), d_in),
        'w_up': nrm((DEPTH, D_MODEL, D_FF), d_in),
        'w_down': nrm((DEPTH, D_FF, D_MODEL), D_FF ** -0.5),
    }


def reference(x_prompt, x_sample, state_conv, cache_k, cache_v, page_table, rel_bias,
              conv_norm, w_pw1, b_pw1, w_dw, b_dw, conv_mid_norm, w_pw2, b_pw2,
              kv_norm, w_kv, k_norm, attn_norm, w_q, q_norm,
              lambda_q1, lambda_k1, lambda_q2, lambda_k2, sub_norm, w_o,
              ffn_norm, w_gate, w_up, w_down):
    hp, hs = x_prompt, x_sample
    conv_zero = jnp.zeros((hp.shape[0], CONV_W - 1, D_MODEL), hp.dtype)
    dbatch, n_pages = page_table.shape
    past = n_pages * cache_k.shape[1]
    k_past = cache_k[page_table].reshape(dbatch, past, N_HEADS, 2, HEAD_DIM)
    v_past = cache_v[page_table].reshape(dbatch, past, N_HEADS, V_DIM)
    conv_p, conv_s = [], []
    for l in range(DEPTH):
        if l < N_A:
            prm = (conv_norm[l], w_pw1[l], b_pw1[l], w_dw[l], b_dw[l], conv_mid_norm[l], w_pw2[l], b_pw2[l])
            dp, sp = conv_module(hp, conv_zero, *prm)
            ds, ss = conv_module(hs, state_conv[l], *prm)
            hp = hp + dp
            hs = hs + ds
            conv_p.append(sp)
            conv_s.append(ss)
        else:
            j = l - N_A
            if j == 0:
                kp, vp = shared_kv(hp, kv_norm, w_kv, k_norm)
                ksm, vsm = shared_kv(hs, kv_norm, w_kv, k_norm)
            lam_init = lambda_init(l)
            lam = (jnp.exp(jnp.sum(lambda_q1[j].astype(jnp.float32) * lambda_k1[j].astype(jnp.float32)))
                   - jnp.exp(jnp.sum(lambda_q2[j].astype(jnp.float32) * lambda_k2[j].astype(jnp.float32)))
                   + lam_init)
            qp = queries(hp, attn_norm[j], w_q[j], q_norm[j])
            qs = queries(hs, attn_norm[j], w_q[j], q_norm[j])
            op = prompt_diff_attn(qp, kp, vp, rel_bias, lam)
            os_ = sample_diff_attn(qs, k_past, v_past, ksm, vsm, rel_bias, lam)
            hp = hp + attn_out(op, sub_norm[j], w_o[j], lam_init)
            hs = hs + attn_out(os_, sub_norm[j], w_o[j], lam_init)
        hp = hp + swiglu(hp, ffn_norm[l], w_gate[l], w_up[l], w_down[l])
        hs = hs + swiglu(hs, ffn_norm[l], w_gate[l], w_up[l], w_down[l])
    conv_state_p = jnp.stack(conv_p)
    conv_state_s = jnp.stack(conv_s)
    return (hp, hs, conv_state_p, conv_state_s, kp, vp, ksm, vsm)
```

```python
import functools
import math

import jax
import jax.numpy as jnp
from jax import lax
from jax.experimental import pallas as pl
from jax.experimental.pallas import tpu as pltpu

D_MODEL = 1024
N_HEADS = 8
HEAD_DIM = 64
V_DIM = 2 * HEAD_DIM
CONV_W = 31
N_BUCKETS = 32
MAX_EXACT = N_BUCKETS // 2
MAX_DISTANCE = 128
EPS = 1e-6
MASKED = -1e30

LANES = 128
SUBLANES = 8
VMEM_LIMIT = 56 * 1024 * 1024

F32 = jnp.float32
BF16 = jnp.bfloat16


def _params(n_grid_dims):
    return pltpu.CompilerParams(
        dimension_semantics=("arbitrary",) * n_grid_dims,
        vmem_limit_bytes=VMEM_LIMIT)


def _resident(shape):
    zeros = (0,) * len(shape)
    return pl.BlockSpec(shape, lambda *_: zeros, pipeline_mode=pl.Buffered(1))


def _rms(x, g):
    return x * lax.rsqrt(jnp.mean(x * x, axis=-1, keepdims=True) + EPS) * g


def _sigmoid(x):
    return 1.0 / (1.0 + jnp.exp(-x))


def _dot(a, b):
    return jnp.dot(a, b, preferred_element_type=F32)


def _dot_nt(a, b):
    return lax.dot_general(a, b, (((1,), (1,)), ((), ())), preferred_element_type=F32)


def _ffn_kernel(h_ref, g_ref, wg_ref, wu_ref, wd_ref, o_ref):
    x = h_ref[...]
    xn = _rms(x, g_ref[...]).astype(BF16)
    gate = _dot(xn, wg_ref[...])
    up = _dot(xn, wu_ref[...])
    act = (gate * _sigmoid(gate) * up).astype(BF16)
    o_ref[...] = x + _dot(act, wd_ref[...])


def _ffn(h, g, wg, wu, wd, *, tm):
    n, d = h.shape
    f = wg.shape[1]
    tok = pl.BlockSpec((tm, d), lambda i: (i, 0))
    return pl.pallas_call(
        _ffn_kernel,
        out_shape=jax.ShapeDtypeStruct((n, d), F32),
        grid=(n // tm,),
        in_specs=[tok, _resident((1, d)), _resident((d, f)), _resident((d, f)),
                  _resident((f, d))],
        out_specs=tok,
        compiler_params=_params(1),
        name="ffn",
    )(h, g, wg, wu, wd)


HALO = 32
CONV_ROWS = 128


def _conv_p_kernel(h_ref, gin_ref, w1_ref, b1_ref, wdw_ref, bdw_ref, gmid_ref, w2_ref, b2_ref,
                   o_ref, st_ref, u_ref, c_ref, *, tile, n_tiles):
    t = pl.program_id(1)
    n_chunks = D_MODEL // LANES
    past = CONV_W - 1

    @pl.when(t == 0)
    def _():
        u_ref[:, 0:HALO, :] = jnp.zeros((n_chunks, HALO, LANES), F32)

    x = h_ref[...]
    xn = _rms(x, gin_ref[...]).astype(BF16)
    a = _dot(xn, w1_ref[...]) + b1_ref[...]
    glu = a[:, :D_MODEL] * _sigmoid(a[:, D_MODEL:])
    for c in range(n_chunks):
        u_ref[c, HALO:HALO + tile, :] = glu[:, c * LANES:(c + 1) * LANES]

    for c in range(n_chunks):
        lanes = slice(c * LANES, (c + 1) * LANES)

        def rows(i, carry, c=c, lanes=lanes):
            r0 = pl.multiple_of(i * CONV_ROWS, CONV_ROWS)
            acc = jnp.zeros((CONV_ROWS, LANES), F32)
            for j in range(CONV_W):
                acc = acc + wdw_ref[j:j + 1, lanes] * u_ref[c, pl.ds(r0 + (HALO - past) + j, CONV_ROWS), :]
            c_ref[pl.ds(r0, CONV_ROWS), lanes] = acc
            return carry

        lax.fori_loop(0, tile // CONV_ROWS, rows, 0)

    @pl.when(t == n_tiles - 1)
    def _():
        for c in range(n_chunks):
            st_ref[:, c * LANES:(c + 1) * LANES] = u_ref[c, HALO + tile - past:HALO + tile, :]

    u_ref[:, 0:HALO, :] = u_ref[:, tile:tile + HALO, :]

    cv = c_ref[...] + bdw_ref[...]
    cn = _rms(cv, gmid_ref[...])
    act = (cn * _sigmoid(cn)).astype(BF16)
    o_ref[...] = x + _dot(act, w2_ref[...]) + b2_ref[...]


def _conv_prompt(h, batch, seq, gin, w1, b1, wdw, bdw, gmid, w2, b2, *, tile):
    d = D_MODEL
    n_tiles = seq // tile
    tok = pl.BlockSpec((tile, d), lambda b, t: (b * n_tiles + t, 0))
    kern = functools.partial(_conv_p_kernel, tile=tile, n_tiles=n_tiles)
    return pl.pallas_call(
        kern,
        out_shape=(jax.ShapeDtypeStruct((batch * seq, d), F32),
                   jax.ShapeDtypeStruct((batch, CONV_W - 1, d), F32)),
        grid=(batch, n_tiles),
        in_specs=[tok, _resident((1, d)), _resident((d, 2 * d)), _resident((1, 2 * d)),
                  _resident((CONV_W, d)), _resident((1, d)), _resident((1, d)),
                  _resident((d, d)), _resident((1, d))],
        out_specs=(tok, pl.BlockSpec((None, CONV_W - 1, d), lambda b, t: (b, 0, 0))),
        scratch_shapes=[pltpu.VMEM((d // LANES, HALO + tile, LANES), F32),
                        pltpu.VMEM((tile, d), F32)],
        compiler_params=_params(2),
        name="conv_prompt",
    )(h, gin, w1, b1, wdw, bdw, gmid, w2, b2)


def _conv_s_kernel(h_ref, st_ref, gin_ref, w1_ref, b1_ref, wdw_ref, bdw_ref, gmid_ref, w2_ref,
                   b2_ref, o_ref, nst_ref, *, bb, t_new):
    past = CONV_W - 1
    x = h_ref[...].reshape(t_new * bb, D_MODEL)
    xn = _rms(x, gin_ref[...]).astype(BF16)
    a = _dot(xn, w1_ref[...]) + b1_ref[...]
    glu = a[:, :D_MODEL] * _sigmoid(a[:, D_MODEL:])

    def u(i):
        return st_ref[i] if i < past else glu[(i - past) * bb:(i - past + 1) * bb]

    outs = []
    for t in range(t_new):
        acc = wdw_ref[0:1, :] * u(t)
        for j in range(1, CONV_W):
            acc = acc + wdw_ref[j:j + 1, :] * u(t + j)
        outs.append(acc)
    for i in range(past):
        nst_ref[i] = u(i + t_new)
    cv = jnp.concatenate(outs, axis=0) + bdw_ref[...]
    cn = _rms(cv, gmid_ref[...])
    act = (cn * _sigmoid(cn)).astype(BF16)
    y = x + _dot(act, w2_ref[...]) + b2_ref[...]
    o_ref[...] = y.reshape(t_new, bb, D_MODEL)


def _conv_sample(h, states, layer, gin, w1, b1, wdw, bdw, gmid, w2, b2, *, bb):
    t_new, n_b, d = h.shape
    past = CONV_W - 1
    tok = pl.BlockSpec((t_new, bb, d), lambda i: (0, i, 0))
    st_in = pl.BlockSpec((None, past, bb, d), lambda i: (layer, 0, i, 0))
    st_out = pl.BlockSpec((past, bb, d), lambda i: (0, i, 0))
    kern = functools.partial(_conv_s_kernel, bb=bb, t_new=t_new)
    return pl.pallas_call(
        kern,
        out_shape=(jax.ShapeDtypeStruct(h.shape, F32), jax.ShapeDtypeStruct((past, n_b, d), F32)),
        grid=(n_b // bb,),
        in_specs=[tok, st_in, _resident((1, d)), _resident((d, 2 * d)), _resident((1, 2 * d)),
                  _resident((CONV_W, d)), _resident((1, d)), _resident((1, d)),
                  _resident((d, d)), _resident((1, d))],
        out_specs=(tok, st_out),
        compiler_params=_params(1),
        name="conv_sample",
    )(h, states, gin, w1, b1, wdw, bdw, gmid, w2, b2)


def _head_rms(xh, g2, lo):
    sq = xh * xh
    s_lo = jnp.sum(jnp.where(lo, sq, 0.0), axis=-1, keepdims=True)
    s_hi = jnp.sum(jnp.where(lo, 0.0, sq), axis=-1, keepdims=True)
    inv = 1.0 / HEAD_DIM
    r = jnp.where(lo, lax.rsqrt(s_lo * inv + EPS), lax.rsqrt(s_hi * inv + EPS))
    return xh * r * g2


def _lo_mask():
    return lax.broadcasted_iota(jnp.int32, (1, V_DIM), 1) < HEAD_DIM


def _kv_kernel(h_ref, g_ref, w_ref, kg_ref, k_ref, v_ref, *attn_copies):
    tm = h_ref.shape[0]
    xn = _rms(h_ref[...], g_ref[...]).astype(BF16)
    kv = _dot(xn, w_ref[...])
    lo = _lo_mask()
    for h in range(N_HEADS):
        lanes = slice(h * V_DIM, (h + 1) * V_DIM)
        kh = _head_rms(kv[:, lanes], kg_ref[...], lo)
        vh = kv[:, D_MODEL + h * V_DIM:D_MODEL + (h + 1) * V_DIM]
        if attn_copies:
            kt_ref, vb_ref = attn_copies
            kht = kh.T
            k_ref[lanes, :] = kht
            kt_ref[h] = kht.astype(BF16)
            v_ref[pl.ds(h, tm, stride=N_HEADS), :] = vh
            vb_ref[h] = vh.astype(BF16)
        else:
            k_ref[:, lanes] = kh
            v_ref[:, lanes] = vh


def _kv_proj(h, g, w, kg2, *, tm, batch=None, seq=None):
    n, d = h.shape
    tok = pl.BlockSpec((tm, d), lambda i: (i, 0))
    if batch is None:
        out_shape = [jax.ShapeDtypeStruct((n, d), F32), jax.ShapeDtypeStruct((n, d), F32)]
        out_specs = [tok, tok]
    else:
        per = seq // tm
        out_shape = [jax.ShapeDtypeStruct((batch, d, seq), F32),
                     jax.ShapeDtypeStruct((n * N_HEADS, V_DIM), F32),
                     jax.ShapeDtypeStruct((batch, N_HEADS, V_DIM, seq), BF16),
                     jax.ShapeDtypeStruct((batch, N_HEADS, seq, V_DIM), BF16)]
        out_specs = [pl.BlockSpec((None, d, tm), lambda i: (i // per, 0, i % per)),
                     pl.BlockSpec((tm * N_HEADS, V_DIM), lambda i: (i, 0)),
                     pl.BlockSpec((None, N_HEADS, V_DIM, tm), lambda i: (i // per, 0, 0, i % per)),
                     pl.BlockSpec((None, N_HEADS, tm, V_DIM), lambda i: (i // per, 0, i % per, 0))]
    return pl.pallas_call(
        _kv_kernel,
        out_shape=tuple(out_shape),
        grid=(n // tm,),
        in_specs=[tok, _resident((1, d)), _resident((d, 2 * d)), _resident((1, V_DIM))],
        out_specs=tuple(out_specs),
        compiler_params=_params(1),
        name="kv_proj",
    )(h, g, w, kg2)


def _q_kernel(h_ref, g_ref, w_ref, qg_ref, q_ref, *, head_major):
    xn = _rms(h_ref[...], g_ref[...]).astype(BF16)
    q = _dot(xn, w_ref[...])
    lo = _lo_mask()
    scale = HEAD_DIM ** -0.5
    for h in range(N_HEADS):
        lanes = slice(h * V_DIM, (h + 1) * V_DIM)
        qh = _head_rms(q[:, lanes], qg_ref[...], lo) * scale
        if head_major:
            q_ref[h] = qh.astype(BF16)
        else:
            q_ref[:, lanes] = qh


def _q_proj(h, g, w, qg2, *, tm, batch=None, seq=None):
    n, d = h.shape
    tok = pl.BlockSpec((tm, d), lambda i: (i, 0))
    if batch is not None:
        per = seq // tm
        out_shape = jax.ShapeDtypeStruct((batch, N_HEADS, seq, V_DIM), BF16)
        out_spec = pl.BlockSpec((None, N_HEADS, tm, V_DIM), lambda i: (i // per, 0, i % per, 0))
    else:
        out_shape = jax.ShapeDtypeStruct((n, d), F32)
        out_spec = tok
    return pl.pallas_call(
        functools.partial(_q_kernel, head_major=batch is not None),
        out_shape=out_shape,
        grid=(n // tm,),
        in_specs=[tok, _resident((1, d)), _resident((d, d)), _resident((1, V_DIM))],
        out_specs=out_spec,
        compiler_params=_params(1),
        name="q_proj",
    )(h, g, w, qg2)


def _bucket(n):
    nf = jnp.maximum(n, 1).astype(F32)
    large = MAX_EXACT + (jnp.log(nf / MAX_EXACT) / math.log(MAX_DISTANCE / MAX_EXACT)
                         * (N_BUCKETS - MAX_EXACT)).astype(jnp.int32)
    large = jnp.minimum(large, N_BUCKETS - 1)
    return jnp.where(n < MAX_EXACT, n, large)


def _lambda(lq1_ref, lk1_ref, lq2_ref, lk2_ref, lam_init):
    s1 = jnp.sum(lq1_ref[...] * lk1_ref[...], axis=-1, keepdims=True)
    s2 = jnp.sum(lq2_ref[...] * lk2_ref[...], axis=-1, keepdims=True)
    return jnp.exp(s1) - jnp.exp(s2) + lam_init


def _attn_p_kernel(rb_ref, lq1_ref, lk1_ref, lq2_ref, lk2_ref, q_ref, k_ref, v_ref, o_ref,
                   bias_ref, m_ref, l_ref, acc_ref, *, blk, lam_init):
    qi = pl.program_id(1)

    @pl.when((pl.program_id(0) == 0) & (qi == 0))
    def _():
        r = lax.broadcasted_iota(jnp.int32, (blk, blk), 0)
        c = lax.broadcasted_iota(jnp.int32, (blk, blk), 1)
        for which in range(2):
            dist = r - c + (blk if which == 0 else 0)
            bucket = _bucket(jnp.maximum(dist, 0))
            for h in range(N_HEADS):
                tile = jnp.zeros((blk, blk), F32)
                for b in range(N_BUCKETS - 1):
                    tile = jnp.where(bucket == b, rb_ref[b, h] - rb_ref[N_BUCKETS - 1, h], tile)
                if which == 1:
                    tile = jnp.where(dist >= 0, tile, MASKED)
                bias_ref[which, h] = tile

    lam = _lambda(lq1_ref, lk1_ref, lq2_ref, lk2_ref, lam_init)
    lane = lax.broadcasted_iota(jnp.int32, (1, V_DIM), 1)
    zero = jnp.zeros((), BF16)

    def head(h, carry):
        qh = q_ref[h]
        q2 = jnp.concatenate([jnp.where(lane < HEAD_DIM, qh, zero),
                              jnp.where(lane < HEAD_DIM, zero, qh)], axis=0)
        m_ref[...] = jnp.full(m_ref.shape, MASKED, F32)
        l_ref[...] = jnp.zeros(l_ref.shape, F32)
        acc_ref[...] = jnp.zeros(acc_ref.shape, F32)

        def block(j, bias):
            start = pl.multiple_of(j * blk, blk)
            s = _dot(q2, k_ref[h, :, pl.ds(start, blk)])
            if bias is not None:
                s = s + jnp.concatenate([bias, bias], axis=0)
            m_old = m_ref[...]
            m_new = jnp.maximum(m_old, jnp.max(s, axis=-1, keepdims=True))
            alpha = jnp.exp(m_old - m_new)
            p = jnp.exp(s - jnp.concatenate([m_new] * (blk // LANES), axis=1))
            l_ref[...] = alpha * l_ref[...] + jnp.sum(p, axis=-1, keepdims=True)
            acc_ref[...] = alpha * acc_ref[...] + _dot(p.astype(BF16), v_ref[h, pl.ds(start, blk), :])
            m_ref[...] = m_new

        def far(j, c):
            block(j, None)
            return c

        lax.fori_loop(0, qi - 1, far, 0)

        @pl.when(qi > 0)
        def _():
            block(qi - 1, bias_ref[0, h])

        block(qi, bias_ref[1, h])
        w = acc_ref[...] / l_ref[...]
        o_ref[h] = w[:blk] - lam * w[blk:]
        return carry

    lax.fori_loop(0, N_HEADS, head, 0)


def _attn_prompt(q, k, v, rel_bias, lam_vecs, lam_init, *, blk):
    batch, _, seq, _ = q.shape
    vec = _resident((1, HEAD_DIM))
    qo = lambda b, i: (b, 0, i, 0)
    whole = lambda b, i: (b, 0, 0, 0)
    kern = functools.partial(_attn_p_kernel, blk=blk, lam_init=lam_init)
    return pl.pallas_call(
        kern,
        out_shape=jax.ShapeDtypeStruct((batch, N_HEADS, seq, V_DIM), F32),
        grid=(batch, seq // blk),
        in_specs=[pl.BlockSpec(memory_space=pltpu.SMEM), vec, vec, vec, vec,
                  pl.BlockSpec((None, N_HEADS, blk, V_DIM), qo),
                  pl.BlockSpec((None, N_HEADS, V_DIM, seq), whole),
                  pl.BlockSpec((None, N_HEADS, seq, V_DIM), whole)],
        out_specs=pl.BlockSpec((None, N_HEADS, blk, V_DIM), qo),
        scratch_shapes=[pltpu.VMEM((2, N_HEADS, blk, blk), F32),
                        pltpu.VMEM((2 * blk, LANES), F32),
                        pltpu.VMEM((2 * blk, LANES), F32),
                        pltpu.VMEM((2 * blk, V_DIM), F32)],
        compiler_params=_params(2),
        name="attn_prompt",
    )(rel_bias, *lam_vecs, q, k, v)


def _attn_s_kernel(pt_ref, rb_ref, lq1_ref, lk1_ref, lq2_ref, lk2_ref, q_ref, kn_ref, vn_ref, *rest,
                   n_pages, page, t_new, lam_init):
    del pt_ref
    k_refs = rest[:n_pages]
    v_refs = rest[n_pages:2 * n_pages]
    o_ref, bp_ref, bn_ref = rest[2 * n_pages:]
    rows = 2 * N_HEADS * t_new
    half = N_HEADS * t_new

    @pl.when(pl.program_id(0) == 0)
    def _():
        def tile(shape, dist):
            r = lax.broadcasted_iota(jnp.int32, shape, 0)
            head = (r % half) // t_new
            bucket = _bucket(jnp.maximum(dist, 0))
            out = jnp.zeros(shape, F32)
            for h in range(N_HEADS):
                for b in range(N_BUCKETS - 1):
                    out = jnp.where((head == h) & (bucket == b),
                                    rb_ref[b, h] - rb_ref[N_BUCKETS - 1, h], out)
            return jnp.where(dist >= 0, out, MASKED)

        r = lax.broadcasted_iota(jnp.int32, (rows, page), 0)
        c = lax.broadcasted_iota(jnp.int32, (rows, page), 1)
        bp_ref[...] = tile((rows, page), page + r % t_new - c)
        r = lax.broadcasted_iota(jnp.int32, (rows, t_new), 0)
        c = lax.broadcasted_iota(jnp.int32, (rows, t_new), 1)
        bn_ref[...] = tile((rows, t_new), r % t_new - c)

    lam = _lambda(lq1_ref, lk1_ref, lq2_ref, lk2_ref, lam_init)

    q = q_ref[...]
    r = lax.broadcasted_iota(jnp.int32, (rows, D_MODEL), 0)
    ln = lax.broadcasted_iota(jnp.int32, (rows, D_MODEL), 1)
    group = ((r % half) // t_new) * 2 + r // half
    q2 = jnp.where(ln // HEAD_DIM == group, jnp.concatenate([q] * (rows // t_new), axis=0), 0.0)
    q2 = q2.astype(BF16)

    past = n_pages * page
    kt_all = jnp.concatenate([k[...].astype(BF16) for k in k_refs], axis=1)
    s_past = _dot(q2, kt_all)
    s_past = jnp.concatenate([s_past[:, :past - page], s_past[:, past - page:] + bp_ref[...]], axis=1)
    s_new = _dot_nt(q2, kn_ref[...].astype(BF16)) + bn_ref[...]

    m = jnp.maximum(jnp.max(s_past, axis=-1, keepdims=True), jnp.max(s_new, axis=-1, keepdims=True))
    p_past = jnp.exp(s_past - m)
    p_new = jnp.exp(s_new - m)
    inv_l = 1.0 / (jnp.sum(p_past, axis=-1, keepdims=True) + jnp.sum(p_new, axis=-1, keepdims=True))
    w_past = p_past * inv_l
    w_new = p_new * inv_l
    a_past = (w_past[:half] - lam * w_past[half:]).astype(BF16)
    a_new = (w_new[:half] - lam * w_new[half:]).astype(BF16)

    def page_values(v):
        return jnp.concatenate([v[pl.ds(h, page, stride=N_HEADS), :].astype(BF16)
                                for h in range(N_HEADS)], axis=1)

    v_all = jnp.concatenate([page_values(v) for v in v_refs], axis=0)
    full = _dot(a_past, v_all) + _dot(a_new, vn_ref[...].astype(BF16))
    for h in range(N_HEADS):
        o_ref[h] = full[h * t_new:(h + 1) * t_new, h * V_DIM:(h + 1) * V_DIM]


def _attn_sample(q, k_new, v_new, cache_kt, cache_v, page_table, rel_bias, lam_vecs, lam_init, *, t_new):
    n_b, n_pages = page_table.shape
    page = cache_kt.shape[2]
    d = D_MODEL
    rows = 2 * N_HEADS * t_new
    vec = pl.BlockSpec((1, HEAD_DIM), lambda b, pt: (0, 0))
    tok = pl.BlockSpec((t_new, d), lambda b, pt: (b, 0))

    def page_spec(i, shape):
        return pl.BlockSpec((None,) + shape, lambda b, pt: (pt[b, i], 0, 0))

    k_pages = [page_spec(i, (d, page)) for i in range(n_pages)]
    v_pages = [page_spec(i, (page * N_HEADS, V_DIM)) for i in range(n_pages)]
    kern = functools.partial(_attn_s_kernel, n_pages=n_pages, page=page, t_new=t_new, lam_init=lam_init)
    grid_spec = pltpu.PrefetchScalarGridSpec(
        num_scalar_prefetch=1,
        grid=(n_b,),
        in_specs=[pl.BlockSpec(memory_space=pltpu.SMEM), vec, vec, vec, vec, tok, tok, tok]
        + k_pages + v_pages,
        out_specs=pl.BlockSpec((N_HEADS, t_new, V_DIM), lambda b, pt: (0, b, 0)),
        scratch_shapes=[pltpu.VMEM((rows, page), F32), pltpu.VMEM((rows, t_new), F32)],
    )
    return pl.pallas_call(
        kern,
        out_shape=jax.ShapeDtypeStruct((N_HEADS, n_b * t_new, V_DIM), F32),
        grid_spec=grid_spec,
        compiler_params=_params(1),
        name="attn_sample",
    )(page_table, rel_bias, *lam_vecs, q, k_new, v_new, *([cache_kt] * n_pages), *([cache_v] * n_pages))


def _attn_out_kernel(o_ref, h_ref, g_ref, w_ref, out_ref, *, scale):
    pieces = []
    for h in range(N_HEADS):
        pieces.append((_rms(o_ref[h], g_ref[...]) * scale).astype(BF16))
    on = jnp.concatenate(pieces, axis=1)
    out_ref[...] = h_ref[...] + _dot(on, w_ref[...])


def _attn_out(o, h, g, w, scale, *, tm):
    groups, _, rows, _ = o.shape
    n, d = h.shape
    per = rows // tm
    tok = pl.BlockSpec((tm, d), lambda i: (i, 0))
    return pl.pallas_call(
        functools.partial(_attn_out_kernel, scale=scale),
        out_shape=jax.ShapeDtypeStruct((n, d), F32),
        grid=(n // tm,),
        in_specs=[pl.BlockSpec((None, N_HEADS, tm, V_DIM), lambda i: (i // per, 0, i % per, 0)),
                  tok, _resident((1, V_DIM)), _resident((d, d))],
        out_specs=tok,
        compiler_params=_params(1),
        name="attn_out",
    )(o, h, g, w)


def _lambda_init(layer_idx):
    return 0.8 - 0.6 * math.exp(-0.3 * layer_idx)


def kernel(x_prompt, x_sample, state_conv, cache_k, cache_v, page_table, rel_bias, conv_norm, w_pw1, b_pw1, w_dw, b_dw, conv_mid_norm, w_pw2, b_pw2, kv_norm, w_kv, k_norm, attn_norm, w_q, q_norm, lambda_q1, lambda_k1, lambda_q2, lambda_k2, sub_norm, w_o, ffn_norm, w_gate, w_up, w_down):
    batch, seq, d = x_prompt.shape
    n_b, t_new, _ = x_sample.shape
    n_a = state_conv.shape[0]
    depth = ffn_norm.shape[0]
    n_pool, page = cache_k.shape[:2]

    row = lambda a: a.reshape(1, -1)
    two = lambda a: jnp.concatenate([a, a]).reshape(1, V_DIM)
    bf = lambda a: a.astype(BF16)

    hp = x_prompt.reshape(batch * seq, d)
    ckt = jnp.transpose(cache_k, (0, 2, 3, 4, 1)).reshape(n_pool, d, page)
    cv = cache_v.reshape(n_pool, page * N_HEADS, V_DIM)
    state_t = jnp.transpose(state_conv, (0, 2, 1, 3))
    hs_t = jnp.transpose(x_sample, (1, 0, 2))

    def ffn_weights(l):
        return row(ffn_norm[l]), bf(w_gate[l]), bf(w_up[l]), bf(w_down[l])

    conv_p, conv_s = [], []
    for l in range(n_a):
        prm = (row(conv_norm[l]), bf(w_pw1[l]), row(b_pw1[l]), w_dw[l], row(b_dw[l]),
               row(conv_mid_norm[l]), bf(w_pw2[l]), row(b_pw2[l]))
        ffn = ffn_weights(l)
        hp, sp = _conv_prompt(hp, batch, seq, *prm, tile=256)
        hs_t, ss = _conv_sample(hs_t, state_t, l, *prm, bb=16)
        conv_p.append(sp)
        conv_s.append(ss)
        hp = _ffn(hp, *ffn, tm=512)
        hs_t = _ffn(hs_t.reshape(t_new * n_b, d), *ffn, tm=256).reshape(t_new, n_b, d)

    hs = jnp.transpose(hs_t, (1, 0, 2)).reshape(n_b * t_new, d)
    wkv = bf(w_kv)
    kpt, vp, kpb, vpb = _kv_proj(hp, row(kv_norm), wkv, two(k_norm), tm=512, batch=batch, seq=seq)
    ksm, vsm = _kv_proj(hs, row(kv_norm), wkv, two(k_norm), tm=256)
    for l in range(n_a, depth):
        j = l - n_a
        lam_init = _lambda_init(l)
        lam_vecs = (row(lambda_q1[j]), row(lambda_k1[j]), row(lambda_q2[j]), row(lambda_k2[j]))
        wq = bf(w_q[j])
        wo = bf(w_o[j])
        ffn = ffn_weights(l)
        qp = _q_proj(hp, row(attn_norm[j]), wq, two(q_norm[j]), tm=512, batch=batch, seq=seq)
        qs = _q_proj(hs, row(attn_norm[j]), wq, two(q_norm[j]), tm=256)
        op = _attn_prompt(qp, kpb, vpb, rel_bias, lam_vecs, lam_init, blk=256)
        os_ = _attn_sample(qs, ksm, vsm, ckt, cv, page_table, rel_bias, lam_vecs, lam_init,
                           t_new=t_new)
        hp = _attn_out(op, hp, row(sub_norm[j]), wo, 1.0 - lam_init, tm=512)
        hs = _attn_out(os_[None], hs, row(sub_norm[j]), wo, 1.0 - lam_init, tm=256)
        hp = _ffn(hp, *ffn, tm=512)
        hs = _ffn(hs, *ffn, tm=256)

    k_prompt = jnp.transpose(kpt.reshape(batch, N_HEADS, 2, HEAD_DIM, seq), (0, 4, 1, 2, 3))
    return (hp.reshape(batch, seq, d), hs.reshape(n_b, t_new, d),
            jnp.stack(conv_p), jnp.transpose(jnp.stack(conv_s), (0, 2, 1, 3)),
            k_prompt, vp.reshape(batch, seq, N_HEADS, V_DIM),
            ksm.reshape(n_b, t_new, N_HEADS, 2, HEAD_DIM), vsm.reshape(n_b, t_new, N_HEADS, V_DIM))
```

```python
import functools
import math

import jax
import jax.numpy as jnp
from jax import lax
from jax.experimental import pallas as pl
from jax.experimental.pallas import tpu as pltpu

D_MODEL = 1024
N_HEADS = 8
HEAD_DIM = 64
V_DIM = 2 * HEAD_DIM
CONV_W = 31
N_BUCKETS = 32
MAX_EXACT = N_BUCKETS // 2
MAX_DISTANCE = 128
EPS = 1e-6
MASKED = -1e30
LOG2E = math.log2(math.e)

LANES = 128
VMEM_LIMIT = 56 * 1024 * 1024

F32 = jnp.float32
BF16 = jnp.bfloat16


def _params(n_grid_dims):
    return pltpu.CompilerParams(
        dimension_semantics=("arbitrary",) * n_grid_dims,
        vmem_limit_bytes=VMEM_LIMIT)


def _resident(shape):
    zeros = (0,) * len(shape)
    return pl.BlockSpec(shape, lambda *_: zeros, pipeline_mode=pl.Buffered(1))


def _rms(x, g):
    return x * lax.rsqrt(jnp.mean(x * x, axis=-1, keepdims=True) + EPS) * g


def _sigmoid(x):
    return 1.0 / (1.0 + jnp.exp(-x))


def _dot(a, b):
    return jnp.dot(a, b, preferred_element_type=F32)


def _dot_nt(a, b):
    return lax.dot_general(a, b, (((1,), (1,)), ((), ())), preferred_element_type=F32)


def _layer_tail_kernel(*refs, scale, has_kv, has_q, feature_major):
    refs = list(refs)
    take = lambda n: [refs.pop(0) for _ in range(n)]
    if scale is not None:
        o_ref, = take(1)
    h_ref, = take(1)
    if scale is not None:
        gs_ref, wo_ref = take(2)
    g_ref, wg_ref, wu_ref, wd_ref = take(4)
    kv_in = take(3) if has_kv else None
    q_in = take(3) if has_q else None
    out_ref, = take(1)
    kv_out = take(4 if feature_major else 2) if has_kv else None
    q_out = take(1) if has_q else None

    x = h_ref[...]
    if scale is not None:
        heads = [(_rms(o_ref[h], gs_ref[...]) * scale).astype(BF16) for h in range(N_HEADS)]
        x = x + _dot(jnp.concatenate(heads, axis=1), wo_ref[...])
    xn = _rms(x, g_ref[...]).astype(BF16)
    gate = _dot(xn, wg_ref[...])
    up = _dot(xn, wu_ref[...])
    act = (gate * _sigmoid(gate) * up).astype(BF16)
    y = x + _dot(act, wd_ref[...])
    out_ref[...] = y
    if has_kv:
        _kv_body(y, *kv_in, *kv_out)
    if has_q:
        _q_body(y, *q_in, *q_out, feature_major=feature_major)


def _resident_layer(shape, layer):
    index = (layer,) + (0,) * len(shape)
    return pl.BlockSpec((None,) + shape, lambda *_: index, pipeline_mode=pl.Buffered(1))


def _layer_tail(h, ffn, *, tm, attn=None, kv=None, q=None, batch=None, seq=None):
    n, d = h.shape
    g, wg, wu, wd, layer = ffn
    f = wg.shape[2]
    tok = pl.BlockSpec((tm, d), lambda i: (i, 0))
    operands, in_specs = [], []
    if attn is not None:
        o, g_sub, wo, scale = attn
        per = o.shape[2] // tm
        operands.append(o)
        in_specs.append(pl.BlockSpec((None, N_HEADS, tm, V_DIM), lambda i: (i // per, 0, i % per, 0)))
    operands.append(h)
    in_specs.append(tok)
    if attn is not None:
        operands += [g_sub, wo]
        in_specs += [_resident((1, V_DIM)), _resident((d, d))]
    operands += [g, wg, wu, wd]
    in_specs += [_resident((1, d)), _resident_layer((d, f), layer), _resident_layer((d, f), layer),
                 _resident_layer((f, d), layer)]
    out_shape, out_specs = [jax.ShapeDtypeStruct((n, d), F32)], [tok]
    if kv is not None:
        operands += list(kv)
        in_specs += [_resident((1, d)), _resident((d, 2 * d)), _resident((1, V_DIM))]
        shapes, specs = _kv_outputs(n, tm, batch, seq)
        out_shape += shapes
        out_specs += specs
    if q is not None:
        operands += list(q)
        in_specs += [_resident((1, d)), _resident((d, d)), _resident((1, V_DIM))]
        shape, spec = _q_outputs(n, tm, batch, seq)
        out_shape.append(shape)
        out_specs.append(spec)
    kern = functools.partial(_layer_tail_kernel, scale=None if attn is None else attn[3],
                             has_kv=kv is not None, has_q=q is not None,
                             feature_major=batch is not None)
    return pl.pallas_call(
        kern,
        out_shape=tuple(out_shape),
        grid=(n // tm,),
        in_specs=in_specs,
        out_specs=tuple(out_specs),
        compiler_params=_params(1),
        name="layer_tail",
    )(*operands)


SUM_ROWS = 16
HALO = 32
CONV_ROWS = 128


def _conv_p_body(t, h_ref, gin_ref, w1_ref, b1_ref, wdw_ref, bdw_ref, gmid_ref, w2_ref, b2_ref,
                 o_ref, st_ref, u_ref, c_ref, *, tile, n_tiles, alongside):
    n_chunks = D_MODEL // LANES
    past = CONV_W - 1

    @pl.when(t == 0)
    def _():
        u_ref[:, 0:HALO, :] = jnp.zeros((n_chunks, HALO, LANES), F32)

    x = h_ref[...]
    xn = _rms(x, gin_ref[...]).astype(BF16)
    a = _dot(xn, w1_ref[...]) + b1_ref[...]
    glu = a[:, :D_MODEL] * _sigmoid(a[:, D_MODEL:])
    for c in range(n_chunks):
        u_ref[c, HALO:HALO + tile, :] = glu[:, c * LANES:(c + 1) * LANES]

    def taps(c, r0):
        lanes = slice(c * LANES, (c + 1) * LANES)
        acc = jnp.zeros((CONV_ROWS, LANES), F32)
        for j in range(CONV_W):
            acc = acc + wdw_ref[j:j + 1, lanes] * u_ref[c, pl.ds(r0 + (HALO - past) + j, CONV_ROWS), :]
        c_ref[pl.ds(r0, CONV_ROWS), lanes] = acc

    alongside()
    for c in range(n_chunks):
        for i in range(tile // CONV_ROWS):
            taps(c, i * CONV_ROWS)

    cv = c_ref[...] + bdw_ref[...]
    cn = _rms(cv, gmid_ref[...])
    act = (cn * _sigmoid(cn)).astype(BF16)
    o_ref[...] = x + _dot(act, w2_ref[...]) + b2_ref[...]

    @pl.when(t == n_tiles - 1)
    def _():
        for c in range(n_chunks):
            st_ref[:, c * LANES:(c + 1) * LANES] = u_ref[c, HALO + tile - past:HALO + tile, :]

    u_ref[:, 0:HALO, :] = u_ref[:, tile:tile + HALO, :]


def _conv_s_kernel(h_ref, st_ref, gin_ref, w1_ref, b1_ref, wdw_ref, bdw_ref, gmid_ref, w2_ref,
                   b2_ref, o_ref, nst_ref, *, bb, t_new):
    past = CONV_W - 1
    x = h_ref[...].reshape(t_new * bb, D_MODEL)
    xn = _rms(x, gin_ref[...]).astype(BF16)
    a = _dot(xn, w1_ref[...]) + b1_ref[...]
    glu = a[:, :D_MODEL] * _sigmoid(a[:, D_MODEL:])

    def u(i):
        return st_ref[i] if i < past else glu[(i - past) * bb:(i - past + 1) * bb]

    outs = []
    for t in range(t_new):
        acc = wdw_ref[0:1, :] * u(t)
        for j in range(1, CONV_W):
            acc = acc + wdw_ref[j:j + 1, :] * u(t + j)
        outs.append(acc)
    for i in range(past):
        nst_ref[i] = u(i + t_new)
    cv = jnp.concatenate(outs, axis=0) + bdw_ref[...]
    cn = _rms(cv, gmid_ref[...])
    act = (cn * _sigmoid(cn)).astype(BF16)
    y = x + _dot(act, w2_ref[...]) + b2_ref[...]
    o_ref[...] = y.reshape(t_new, bb, D_MODEL)


def _conv_sample(h, states, layer, gin, w1, b1, wdw, bdw, gmid, w2, b2, *, bb):
    t_new, n_b, d = h.shape
    past = CONV_W - 1
    tok = pl.BlockSpec((t_new, bb, d), lambda i: (0, i, 0))
    st_in = pl.BlockSpec((None, past, bb, d), lambda i: (layer, 0, i, 0))
    st_out = pl.BlockSpec((past, bb, d), lambda i: (0, i, 0))
    kern = functools.partial(_conv_s_kernel, bb=bb, t_new=t_new)
    return pl.pallas_call(
        kern,
        out_shape=(jax.ShapeDtypeStruct(h.shape, F32), jax.ShapeDtypeStruct((past, n_b, d), F32)),
        grid=(n_b // bb,),
        in_specs=[tok, st_in, _resident((1, d)), _resident((d, 2 * d)), _resident((1, 2 * d)),
                  _resident((CONV_W, d)), _resident((1, d)), _resident((1, d)),
                  _resident((d, d)), _resident((1, d))],
        out_specs=(tok, st_out),
        compiler_params=_params(1),
        name="conv_sample",
    )(h, states, gin, w1, b1, wdw, bdw, gmid, w2, b2)


def _head_rms(xh, g2, lo):
    sq = xh * xh
    s_lo = jnp.sum(jnp.where(lo, sq, 0.0), axis=-1, keepdims=True)
    s_hi = jnp.sum(jnp.where(lo, 0.0, sq), axis=-1, keepdims=True)
    inv = 1.0 / HEAD_DIM
    r = jnp.where(lo, lax.rsqrt(s_lo * inv + EPS), lax.rsqrt(s_hi * inv + EPS))
    return xh * r * g2


def _lo_mask():
    return lax.broadcasted_iota(jnp.int32, (1, V_DIM), 1) < HEAD_DIM


def _kv_body(x, g_ref, w_ref, kg_ref, k_ref, v_ref, *attn_copies):
    tm = x.shape[0]
    xn = _rms(x, g_ref[...]).astype(BF16)
    kv = _dot(xn, w_ref[...])
    lo = _lo_mask()
    for h in range(N_HEADS):
        lanes = slice(h * V_DIM, (h + 1) * V_DIM)
        kh = _head_rms(kv[:, lanes], kg_ref[...], lo)
        vh = kv[:, D_MODEL + h * V_DIM:D_MODEL + (h + 1) * V_DIM]
        if attn_copies:
            kb_ref, vt_ref = attn_copies
            k_ref[lanes, :] = kh.T
            kb_ref[h] = kh.astype(BF16)
            v_ref[pl.ds(h, tm, stride=N_HEADS), :] = vh
            vt_ref[h] = jnp.concatenate([vh.T.astype(BF16), jnp.ones((SUM_ROWS, tm), BF16)], axis=0)
        else:
            k_ref[:, lanes] = kh
            v_ref[:, lanes] = vh


def _kv_outputs(n, tm, batch, seq):
    d = D_MODEL
    tok = pl.BlockSpec((tm, d), lambda i: (i, 0))
    if batch is None:
        return [jax.ShapeDtypeStruct((n, d), F32)] * 2, [tok, tok]
    per = seq // tm
    shapes = [jax.ShapeDtypeStruct((batch, d, seq), F32),
              jax.ShapeDtypeStruct((n * N_HEADS, V_DIM), F32),
              jax.ShapeDtypeStruct((batch, N_HEADS, seq, V_DIM), BF16),
              jax.ShapeDtypeStruct((batch, N_HEADS, V_DIM + SUM_ROWS, seq), BF16)]
    specs = [pl.BlockSpec((None, d, tm), lambda i: (i // per, 0, i % per)),
             pl.BlockSpec((tm * N_HEADS, V_DIM), lambda i: (i, 0)),
             pl.BlockSpec((None, N_HEADS, tm, V_DIM), lambda i: (i // per, 0, i % per, 0)),
             pl.BlockSpec((None, N_HEADS, V_DIM + SUM_ROWS, tm), lambda i: (i // per, 0, 0, i % per))]
    return shapes, specs


def _q_body(x, g_ref, w_ref, qg_ref, q_ref, *, feature_major):
    xn = _rms(x, g_ref[...]).astype(BF16)
    q = _dot(xn, w_ref[...])
    lo = _lo_mask()
    scale = HEAD_DIM ** -0.5
    for h in range(N_HEADS):
        lanes = slice(h * V_DIM, (h + 1) * V_DIM)
        qh = _head_rms(q[:, lanes], qg_ref[...], lo) * scale
        if feature_major:
            q_ref[h] = (qh * LOG2E).T.astype(BF16)
        else:
            q_ref[:, lanes] = qh


def _q_outputs(n, tm, batch, seq):
    if batch is None:
        return jax.ShapeDtypeStruct((n, D_MODEL), F32), pl.BlockSpec((tm, D_MODEL), lambda i: (i, 0))
    per = seq // tm
    return (jax.ShapeDtypeStruct((batch, N_HEADS, V_DIM, seq), BF16),
            pl.BlockSpec((None, N_HEADS, V_DIM, tm), lambda i: (i // per, 0, 0, i % per)))


def _bucket(n):
    nf = jnp.maximum(n, 1).astype(F32)
    large = MAX_EXACT + (jnp.log(nf / MAX_EXACT) / math.log(MAX_DISTANCE / MAX_EXACT)
                         * (N_BUCKETS - MAX_EXACT)).astype(jnp.int32)
    large = jnp.minimum(large, N_BUCKETS - 1)
    return jnp.where(n < MAX_EXACT, n, large)


def _lambda(lq1_ref, lk1_ref, lq2_ref, lk2_ref, lam_init):
    s1 = jnp.sum(lq1_ref[...] * lk1_ref[...], axis=-1, keepdims=True)
    s2 = jnp.sum(lq2_ref[...] * lk2_ref[...], axis=-1, keepdims=True)
    return jnp.exp(s1) - jnp.exp(s2) + lam_init


LOGITS_AHEAD = 2
N_ATTN_IN = 8


def _attn_p_body(first, qi, rb_ref, lq1_ref, lk1_ref, lq2_ref, lk2_ref, qt_ref, k_ref, vt_ref, o_ref,
                 bias_ref, q2_ref, m_ref, acc_ref, *, blk, lam_init):
    @pl.when(first)
    def _():
        kr = lax.broadcasted_iota(jnp.int32, (blk, blk), 0)
        qc = lax.broadcasted_iota(jnp.int32, (blk, blk), 1)
        for which in range(2):
            dist = qc - kr + (blk if which == 0 else 0)
            bucket = _bucket(jnp.maximum(dist, 0))
            for h in range(N_HEADS):
                tile = jnp.zeros((blk, blk), F32)
                for b in range(N_BUCKETS - 1):
                    tile = jnp.where(bucket == b,
                                     (rb_ref[b, h] - rb_ref[N_BUCKETS - 1, h]) * LOG2E, tile)
                if which == 1:
                    tile = jnp.where(dist >= 0, tile, MASKED)
                bias_ref[which, h] = tile

    lam = _lambda(lq1_ref, lk1_ref, lq2_ref, lk2_ref, lam_init)
    row = lax.broadcasted_iota(jnp.int32, (V_DIM, 1), 0)
    zero = jnp.zeros((), BF16)

    for h in range(N_HEADS):
        qt = qt_ref[h]
        q2_ref[h] = jnp.concatenate([jnp.where(row < HEAD_DIM, qt, zero),
                                     jnp.where(row < HEAD_DIM, zero, qt)], axis=1)
    m_ref[...] = jnp.full(m_ref.shape, MASKED, F32)
    acc_ref[...] = jnp.zeros(acc_ref.shape, F32)

    def logits(h, start, which):
        s = _dot(k_ref[h, pl.ds(start, blk), :], q2_ref[h])
        if which is not None:
            bias = bias_ref[which, h]
            s = s + jnp.concatenate([bias, bias], axis=1)
        return s

    def update(h, start, s):
        m_old = m_ref[h]
        m_new = jnp.maximum(m_old, jnp.max(s, axis=0, keepdims=True))
        alpha = jnp.exp2(m_old - m_new)
        p = jnp.exp2(s - m_new).astype(BF16)
        acc_ref[h] = alpha * acc_ref[h] + _dot(vt_ref[h, :, pl.ds(start, blk)], p)
        m_ref[h] = m_new

    def key_blocks(*blocks):
        units = [(h, pl.multiple_of(j * blk, blk), which) for j, which in blocks for h in range(N_HEADS)]
        pending = [logits(*u) for u in units[:LOGITS_AHEAD]]
        for n, (h, start, _) in enumerate(units):
            if n + LOGITS_AHEAD < len(units):
                pending.append(logits(*units[n + LOGITS_AHEAD]))
            update(h, start, pending.pop(0))

    n_far = jnp.maximum(qi - 1, 0)

    def far_pair(i, c):
        key_blocks((2 * i, None), (2 * i + 1, None))
        return c

    lax.fori_loop(0, lax.shift_right_logical(n_far, 1), far_pair, 0)

    @pl.when(n_far & 1 == 1)
    def _():
        key_blocks((n_far - 1, None))

    @pl.when(qi > 0)
    def _():
        key_blocks((qi - 1, 0), (qi, 1))

    @pl.when(qi == 0)
    def _():
        key_blocks((qi, 1))

    for h in range(N_HEADS):
        w = acc_ref[h, :V_DIM, :] / acc_ref[h, V_DIM:V_DIM + 1, :]
        o_ref[h] = (w[:, :blk] - lam * w[:, blk:]).T


N_DECODE_IN = 8
DECODE_PARTS = 2


def _decode_init(first, inputs, scratch, *, page, t_new):
    rb_ref = inputs[0]
    bp_ref, bn_ref = scratch
    rows = 2 * N_HEADS * t_new
    half = N_HEADS * t_new

    @pl.when(first)
    def _():
        def tile(shape, dist):
            r = lax.broadcasted_iota(jnp.int32, shape, 0)
            head = (r % half) // t_new
            bucket = _bucket(jnp.maximum(dist, 0))
            out = jnp.zeros(shape, F32)
            for h in range(N_HEADS):
                for b in range(N_BUCKETS - 1):
                    out = jnp.where((head == h) & (bucket == b),
                                    rb_ref[b, h] - rb_ref[N_BUCKETS - 1, h], out)
            return jnp.where(dist >= 0, out, MASKED)

        r = lax.broadcasted_iota(jnp.int32, (rows, page), 0)
        c = lax.broadcasted_iota(jnp.int32, (rows, page), 1)
        bp_ref[...] = tile((rows, page), page + r % t_new - c)
        r = lax.broadcasted_iota(jnp.int32, (rows, t_new), 0)
        c = lax.broadcasted_iota(jnp.int32, (rows, t_new), 1)
        bn_ref[...] = tile((rows, t_new), r % t_new - c)


def _decode_main(inputs, o_ref, scratch, *, n_pages, page, t_new, lam_init):
    _, lq1_ref, lk1_ref, lq2_ref, lk2_ref, q_ref, kn_ref, vn_ref = inputs[:N_DECODE_IN]
    k_refs = inputs[N_DECODE_IN:N_DECODE_IN + n_pages]
    v_refs = inputs[N_DECODE_IN + n_pages:]
    bp_ref, bn_ref = scratch
    rows = 2 * N_HEADS * t_new
    half = N_HEADS * t_new
    lam = _lambda(lq1_ref, lk1_ref, lq2_ref, lk2_ref, lam_init)

    q = q_ref[...]
    r = lax.broadcasted_iota(jnp.int32, (rows, D_MODEL), 0)
    ln = lax.broadcasted_iota(jnp.int32, (rows, D_MODEL), 1)
    group = ((r % half) // t_new) * 2 + r // half
    q2 = jnp.where(ln // HEAD_DIM == group, jnp.concatenate([q] * (rows // t_new), axis=0), 0.0)
    q2 = q2.astype(BF16)

    past = n_pages * page

    def page_values(v):
        return jnp.concatenate([v[pl.ds(h, page, stride=N_HEADS), :].astype(BF16)
                                for h in range(N_HEADS)], axis=1)

    per = n_pages // DECODE_PARTS
    parts = [slice(i * per, (i + 1) * per) for i in range(DECODE_PARTS)]
    s_part, v_part = [], []
    for part in parts:
        kt = jnp.concatenate([k[...].astype(BF16) for k in k_refs[part]], axis=1)
        s_part.append(_dot(q2, kt))
        v_part.append(jnp.concatenate([page_values(v) for v in v_refs[part]], axis=0))
    s_past = jnp.concatenate(s_part, axis=1)
    s_past = jnp.concatenate([s_past[:, :past - page], s_past[:, past - page:] + bp_ref[...]], axis=1)
    s_new = _dot_nt(q2, kn_ref[...].astype(BF16)) + bn_ref[...]

    m = jnp.maximum(jnp.max(s_past, axis=-1, keepdims=True), jnp.max(s_new, axis=-1, keepdims=True))
    p_past = jnp.exp(s_past - m)
    p_new = jnp.exp(s_new - m)
    inv_l = 1.0 / (jnp.sum(p_past, axis=-1, keepdims=True) + jnp.sum(p_new, axis=-1, keepdims=True))
    w_past = p_past * inv_l
    w_new = p_new * inv_l
    a_past = (w_past[:half] - lam * w_past[half:]).astype(BF16)
    a_new = (w_new[:half] - lam * w_new[half:]).astype(BF16)

    keys = per * page
    full = _dot(a_new, vn_ref[...].astype(BF16))
    for i, v in enumerate(v_part):
        full = full + _dot(a_past[:, i * keys:(i + 1) * keys], v)
    for h in range(N_HEADS):
        o_ref[h] = full[h * t_new:(h + 1) * t_new, h * V_DIM:(h + 1) * V_DIM]


def _decode_specs(seq_of, out_of, n_pages, page, t_new):
    d = D_MODEL
    rows = 2 * N_HEADS * t_new
    vec = pl.BlockSpec((1, HEAD_DIM), lambda *a: (0, 0))
    tok = pl.BlockSpec((t_new, d), lambda *a: (seq_of(*a[:-1]), 0))

    def page_spec(i, shape):
        return pl.BlockSpec((None,) + shape, lambda *a: (a[-1][seq_of(*a[:-1]), i], 0, 0))

    in_specs = [pl.BlockSpec(memory_space=pltpu.SMEM), vec, vec, vec, vec, tok, tok, tok]
    in_specs += [page_spec(i, (d, page)) for i in range(n_pages)]
    in_specs += [page_spec(i, (page * N_HEADS, V_DIM)) for i in range(n_pages)]
    out_spec = pl.BlockSpec((N_HEADS, t_new, V_DIM), lambda *a: (0, out_of(*a[:-1]), 0))
    scratch = [pltpu.VMEM((rows, page), F32), pltpu.VMEM((rows, t_new), F32)]
    return in_specs, out_spec, scratch


def _decode_operands(q, k_new, v_new, cache_kt, cache_v, rel_bias, lam_vecs, n_pages):
    return (rel_bias, *lam_vecs, q, k_new, v_new, *([cache_kt] * n_pages), *([cache_v] * n_pages))


N_CONV_IN = 9


def _conv_decode_kernel(pt_ref, *refs, tile, n_tiles, n_pages, page, t_new, lam_init):
    del pt_ref
    n_dec = N_DECODE_IN + 2 * n_pages
    conv_in, dec_in = refs[:N_CONV_IN], refs[N_CONV_IN:N_CONV_IN + n_dec]
    o_ref, st_ref, od_ref, u_ref, c_ref, bp_ref, bn_ref = refs[N_CONV_IN + n_dec:]
    first = (pl.program_id(0) == 0) & (pl.program_id(1) == 0)
    _decode_init(first, dec_in, (bp_ref, bn_ref), page=page, t_new=t_new)
    decode = functools.partial(_decode_main, dec_in, od_ref, (bp_ref, bn_ref),
                               n_pages=n_pages, page=page, t_new=t_new, lam_init=lam_init)
    _conv_p_body(pl.program_id(1), *conv_in, o_ref, st_ref, u_ref, c_ref, tile=tile, n_tiles=n_tiles,
                 alongside=decode)


def _conv_prompt_decode(h, batch, seq, conv_params, decode_operands, page_table, seq_offset,
                        lam_init, *, tile, t_new):
    d = D_MODEL
    n_tiles = seq // tile
    n_steps = batch * n_tiles
    n_pages = page_table.shape[1]
    page = decode_operands[-1].shape[1] // N_HEADS
    tok = pl.BlockSpec((tile, d), lambda b, t, pt: (b * n_tiles + t, 0))
    conv_specs = [tok, _resident((1, d)), _resident((d, 2 * d)), _resident((1, 2 * d)),
                  _resident((CONV_W, d)), _resident((1, d)), _resident((1, d)),
                  _resident((d, d)), _resident((1, d))]
    dec_specs, dec_out, dec_scratch = _decode_specs(
        lambda b, t: seq_offset + b * n_tiles + t, lambda b, t: b * n_tiles + t, n_pages, page, t_new)
    kern = functools.partial(_conv_decode_kernel, tile=tile, n_tiles=n_tiles, n_pages=n_pages,
                             page=page, t_new=t_new, lam_init=lam_init)
    grid_spec = pltpu.PrefetchScalarGridSpec(
        num_scalar_prefetch=1, grid=(batch, n_tiles),
        in_specs=conv_specs + dec_specs,
        out_specs=(tok, pl.BlockSpec((None, CONV_W - 1, d), lambda b, t, pt: (b, 0, 0)), dec_out),
        scratch_shapes=[pltpu.VMEM((d // LANES, HALO + tile, LANES), F32),
                        pltpu.VMEM((tile, d), F32)] + dec_scratch)
    return pl.pallas_call(
        kern,
        out_shape=(jax.ShapeDtypeStruct((batch * seq, d), F32),
                   jax.ShapeDtypeStruct((batch, CONV_W - 1, d), F32),
                   jax.ShapeDtypeStruct((N_HEADS, n_steps * t_new, V_DIM), F32)),
        grid_spec=grid_spec,
        compiler_params=_params(2),
        name="conv_prompt_decode",
    )(page_table, h, *conv_params, *decode_operands)


def _attn_decode_kernel(pt_ref, *refs, blk, lam_init, n_pages, page, t_new, lam_init_decode):
    del pt_ref
    n_dec = N_DECODE_IN + 2 * n_pages
    attn_in, dec_in = refs[:N_ATTN_IN], refs[N_ATTN_IN:N_ATTN_IN + n_dec]
    o_ref, od_ref, bias_ref, q2_ref, m_ref, acc_ref, bp_ref, bn_ref = refs[N_ATTN_IN + n_dec:]
    first = (pl.program_id(0) == 0) & (pl.program_id(1) == 0)
    _decode_init(first, dec_in, (bp_ref, bn_ref), page=page, t_new=t_new)
    _decode_main(dec_in, od_ref, (bp_ref, bn_ref),
                 n_pages=n_pages, page=page, t_new=t_new, lam_init=lam_init_decode)
    _attn_p_body(first, pl.program_id(1), *attn_in, o_ref, bias_ref, q2_ref, m_ref, acc_ref,
                 blk=blk, lam_init=lam_init)


def _attn_prompt_decode(qt, k, vt, rel_bias, lam_vecs, lam_init, decode_operands, page_table,
                        seq_offset, lam_init_decode, *, blk, t_new):
    batch, _, _, seq = qt.shape
    n_blk = seq // blk
    n_pages = page_table.shape[1]
    page = decode_operands[-1].shape[1] // N_HEADS
    vec = _resident((1, HEAD_DIM))
    once = dict(pipeline_mode=pl.Buffered(1))
    attn_specs = [pl.BlockSpec(memory_space=pltpu.SMEM), vec, vec, vec, vec,
                  pl.BlockSpec((None, N_HEADS, V_DIM, blk), lambda b, i, pt: (b, 0, 0, i)),
                  pl.BlockSpec((None, N_HEADS, seq, V_DIM), lambda b, i, pt: (b, 0, 0, 0), **once),
                  pl.BlockSpec((None, N_HEADS, V_DIM + SUM_ROWS, seq), lambda b, i, pt: (b, 0, 0, 0), **once)]
    dec_specs, dec_out, dec_scratch = _decode_specs(
        lambda b, i: seq_offset + b * n_blk + i, lambda b, i: b * n_blk + i, n_pages, page, t_new)
    kern = functools.partial(_attn_decode_kernel, blk=blk, lam_init=lam_init, n_pages=n_pages,
                             page=page, t_new=t_new, lam_init_decode=lam_init_decode)
    grid_spec = pltpu.PrefetchScalarGridSpec(
        num_scalar_prefetch=1, grid=(batch, n_blk),
        in_specs=attn_specs + dec_specs,
        out_specs=(pl.BlockSpec((None, N_HEADS, blk, V_DIM), lambda b, i, pt: (b, 0, i, 0)), dec_out),
        scratch_shapes=[pltpu.VMEM((2, N_HEADS, blk, blk), F32),
                        pltpu.VMEM((N_HEADS, V_DIM, 2 * blk), BF16),
                        pltpu.VMEM((N_HEADS, 1, 2 * blk), F32),
                        pltpu.VMEM((N_HEADS, V_DIM + SUM_ROWS, 2 * blk), F32)]
        + dec_scratch)
    return pl.pallas_call(
        kern,
        out_shape=(jax.ShapeDtypeStruct((batch, N_HEADS, seq, V_DIM), F32),
                   jax.ShapeDtypeStruct((N_HEADS, batch * n_blk * t_new, V_DIM), F32)),
        grid_spec=grid_spec,
        compiler_params=_params(2),
        name="attn_prompt_decode",
    )(page_table, rel_bias, *lam_vecs, qt, k, vt, *decode_operands)


def _lambda_init(layer_idx):
    return 0.8 - 0.6 * math.exp(-0.3 * layer_idx)


def kernel(x_prompt, x_sample, state_conv, cache_k, cache_v, page_table, rel_bias, conv_norm, w_pw1, b_pw1, w_dw, b_dw, conv_mid_norm, w_pw2, b_pw2, kv_norm, w_kv, k_norm, attn_norm, w_q, q_norm, lambda_q1, lambda_k1, lambda_q2, lambda_k2, sub_norm, w_o, ffn_norm, w_gate, w_up, w_down):
    batch, seq, d = x_prompt.shape
    n_b, t_new, _ = x_sample.shape
    n_a = state_conv.shape[0]
    depth = ffn_norm.shape[0]
    n_pool, page = cache_k.shape[:2]

    row = lambda a: a.reshape(1, -1)
    two = lambda a: jnp.concatenate([a, a]).reshape(1, V_DIM)
    bf = lambda a: a.astype(BF16)

    hp = x_prompt.reshape(batch * seq, d)
    ckt = jnp.transpose(cache_k, (0, 2, 3, 4, 1)).reshape(n_pool, d, page)
    cv = cache_v.reshape(n_pool, page * N_HEADS, V_DIM)
    state_t = jnp.transpose(state_conv, (0, 2, 1, 3))
    hs_t = jnp.transpose(x_sample, (1, 0, 2))

    wg_all, wu_all, wd_all = bf(w_gate), bf(w_up), bf(w_down)

    def ffn_weights(l):
        return row(ffn_norm[l]), wg_all, wu_all, wd_all, l

    def conv_weights(l):
        return (row(conv_norm[l]), bf(w_pw1[l]), row(b_pw1[l]), w_dw[l], row(b_dw[l]),
                row(conv_mid_norm[l]), bf(w_pw2[l]), row(b_pw2[l]))

    n_attn = depth - n_a
    n_pages = page_table.shape[1]
    lam_vecs = [(row(lambda_q1[j]), row(lambda_k1[j]), row(lambda_q2[j]), row(lambda_k2[j]))
                for j in range(n_attn)]
    lam_inits = [_lambda_init(n_a + j) for j in range(n_attn)]
    kv_weights = (row(kv_norm), bf(w_kv), two(k_norm))

    def q_weights(j):
        return row(attn_norm[j]), bf(w_q[j]), two(q_norm[j])

    def attn_out_weights(o, j):
        return o, row(sub_norm[j]), bf(w_o[j]), 1.0 - lam_inits[j]

    assert n_a >= 1 and n_attn == 2, "kernel schedule below is laid out for conv layers then two attention layers"
    conv_tile = attn_blk = 256
    steps = batch * (seq // conv_tile)
    assert n_a * steps == n_b and n_attn * steps == n_b, "one sample sequence's decode per prompt grid step"
    prompt = dict(tm=512, batch=batch, seq=seq)
    sample = dict(tm=512)

    conv_s = []
    for l in range(n_a):
        hs_t, ss = _conv_sample(hs_t, state_t, l, *conv_weights(l), bb=16)
        conv_s.append(ss)
        if l < n_a - 1:
            hs_t, = _layer_tail(hs_t.reshape(t_new * n_b, d), ffn_weights(l), **sample)
            hs_t = hs_t.reshape(t_new, n_b, d)
    hs = jnp.transpose(hs_t, (1, 0, 2)).reshape(n_b * t_new, d)
    hs, ksm, vsm, qs = _layer_tail(hs, ffn_weights(n_a - 1), kv=kv_weights, q=q_weights(0), **sample)
    decode = _decode_operands(qs, ksm, vsm, ckt, cv, rel_bias, lam_vecs[0], n_pages)

    conv_p, os_parts = [], []
    for l in range(n_a):
        hp, sp, od = _conv_prompt_decode(hp, batch, seq, conv_weights(l), decode, page_table,
                                         l * steps, lam_inits[0], tile=conv_tile, t_new=t_new)
        conv_p.append(sp)
        os_parts.append(od)
        if l < n_a - 1:
            hp, = _layer_tail(hp, ffn_weights(l), **prompt)
    hp, kpt, vp, kpb, vpt, qp = _layer_tail(hp, ffn_weights(n_a - 1), kv=kv_weights, q=q_weights(0),
                                            **prompt)
    os_ = jnp.concatenate(os_parts, axis=1)

    hs, qs = _layer_tail(hs, ffn_weights(n_a), attn=attn_out_weights(os_[None], 0), q=q_weights(1),
                         **sample)
    decode = _decode_operands(qs, ksm, vsm, ckt, cv, rel_bias, lam_vecs[1], n_pages)
    os_parts = []
    for j in range(n_attn):
        op, od = _attn_prompt_decode(qp, kpb, vpt, rel_bias, lam_vecs[j], lam_inits[j], decode,
                                     page_table, j * steps, lam_inits[1], blk=attn_blk, t_new=t_new)
        os_parts.append(od)
        if j + 1 < n_attn:
            hp, qp = _layer_tail(hp, ffn_weights(n_a + j), attn=attn_out_weights(op, j),
                                 q=q_weights(j + 1), **prompt)
        else:
            hp, = _layer_tail(hp, ffn_weights(n_a + j), attn=attn_out_weights(op, j), **prompt)
    hs, = _layer_tail(hs, ffn_weights(n_a + 1), attn=attn_out_weights(jnp.concatenate(os_parts, axis=1)[None], 1),
                      **sample)

    k_prompt = jnp.transpose(kpt.reshape(batch, N_HEADS, 2, HEAD_DIM, seq), (0, 4, 1, 2, 3))
    return (hp.reshape(batch, seq, d), hs.reshape(n_b, t_new, d),
            jnp.stack(conv_p), jnp.transpose(jnp.stack(conv_s), (0, 2, 1, 3)),
            k_prompt, vp.reshape(batch, seq, N_HEADS, V_DIM),
            ksm.reshape(n_b, t_new, N_HEADS, 2, HEAD_DIM), vsm.reshape(n_b, t_new, N_HEADS, V_DIM))
```

```python
import functools
import math

import jax
import jax.numpy as jnp
from jax import lax
from jax.experimental import pallas as pl
from jax.experimental.pallas import tpu as pltpu

D_MODEL = 1024
N_HEADS = 8
HEAD_DIM = 64
V_DIM = 2 * HEAD_DIM
CONV_W = 31
N_BUCKETS = 32
MAX_EXACT = N_BUCKETS // 2
MAX_DISTANCE = 128
EPS = 1e-6
MASKED = -1e30
LOG2E = math.log2(math.e)

LANES = 128
VMEM_LIMIT = 56 * 1024 * 1024

F32 = jnp.float32
BF16 = jnp.bfloat16


def _params(n_grid_dims):
    return pltpu.CompilerParams(
        dimension_semantics=("arbitrary",) * n_grid_dims,
        vmem_limit_bytes=VMEM_LIMIT)


def _resident(shape):
    zeros = (0,) * len(shape)
    return pl.BlockSpec(shape, lambda *_: zeros, pipeline_mode=pl.Buffered(1))


def _rms(x, g):
    return x * lax.rsqrt(jnp.mean(x * x, axis=-1, keepdims=True) + EPS) * g


def _sigmoid(x):
    return 1.0 / (1.0 + jnp.exp(-x))


def _dot(a, b):
    return jnp.dot(a, b, preferred_element_type=F32)


def _dot_nt(a, b):
    return lax.dot_general(a, b, (((1,), (1,)), ((), ())), preferred_element_type=F32)


def _layer_tail_kernel(*refs, scale, has_kv, has_q, feature_major):
    refs = list(refs)
    take = lambda n: [refs.pop(0) for _ in range(n)]
    if scale is not None:
        o_ref, = take(1)
    h_ref, = take(1)
    if scale is not None:
        gs_ref, wo_ref = take(2)
    g_ref, wg_ref, wu_ref, wd_ref = take(4)
    kv_in = take(3) if has_kv else None
    q_in = take(3) if has_q else None
    out_ref, = take(1)
    kv_out = take(4 if feature_major else 2) if has_kv else None
    q_out = take(1) if has_q else None

    x = h_ref[...]
    if scale is not None:
        heads = [(_rms(o_ref[h], gs_ref[...]) * scale).astype(BF16) for h in range(N_HEADS)]
        x = x + _dot(jnp.concatenate(heads, axis=1), wo_ref[...])
    xn = _rms(x, g_ref[...]).astype(BF16)
    gate = _dot(xn, wg_ref[...])
    up = _dot(xn, wu_ref[...])
    act = (gate * _sigmoid(gate) * up).astype(BF16)
    y = x + _dot(act, wd_ref[...])
    out_ref[...] = y
    if has_kv:
        _kv_body(y, *kv_in, *kv_out)
    if has_q:
        _q_body(y, *q_in, *q_out, feature_major=feature_major)


def _resident_layer(shape, layer):
    index = (layer,) + (0,) * len(shape)
    return pl.BlockSpec((None,) + shape, lambda *_: index, pipeline_mode=pl.Buffered(1))


def _layer_tail(h, ffn, *, tm, attn=None, kv=None, q=None, batch=None, seq=None):
    n, d = h.shape
    g, wg, wu, wd, layer = ffn
    f = wg.shape[2]
    tok = pl.BlockSpec((tm, d), lambda i: (i, 0))
    operands, in_specs = [], []
    if attn is not None:
        o, g_sub, wo, scale = attn
        per = o.shape[2] // tm
        operands.append(o)
        in_specs.append(pl.BlockSpec((None, N_HEADS, tm, V_DIM), lambda i: (i // per, 0, i % per, 0)))
    operands.append(h)
    in_specs.append(tok)
    if attn is not None:
        operands += [g_sub, wo]
        in_specs += [_resident((1, V_DIM)), _resident((d, d))]
    operands += [g, wg, wu, wd]
    in_specs += [_resident((1, d)), _resident_layer((d, f), layer), _resident_layer((d, f), layer),
                 _resident_layer((f, d), layer)]
    out_shape, out_specs = [jax.ShapeDtypeStruct((n, d), F32)], [tok]
    if kv is not None:
        operands += list(kv)
        in_specs += [_resident((1, d)), _resident((d, 2 * d)), _resident((1, V_DIM))]
        shapes, specs = _kv_outputs(n, tm, batch, seq)
        out_shape += shapes
        out_specs += specs
    if q is not None:
        operands += list(q)
        in_specs += [_resident((1, d)), _resident((d, d)), _resident((1, V_DIM))]
        shape, spec = _q_outputs(n, tm, batch, seq)
        out_shape.append(shape)
        out_specs.append(spec)
    kern = functools.partial(_layer_tail_kernel, scale=None if attn is None else attn[3],
                             has_kv=kv is not None, has_q=q is not None,
                             feature_major=batch is not None)
    return pl.pallas_call(
        kern,
        out_shape=tuple(out_shape),
        grid=(n // tm,),
        in_specs=in_specs,
        out_specs=tuple(out_specs),
        compiler_params=_params(1),
        name="layer_tail",
    )(*operands)


SUM_ROWS = 16
HALO = 32
CONV_ROWS = 128


def _conv_p_body(t, h_ref, gin_ref, w1_ref, b1_ref, wdw_ref, bdw_ref, gmid_ref, w2_ref, b2_ref,
                 o_ref, st_ref, u_ref, c_ref, *, tile, n_tiles, alongside):
    n_chunks = D_MODEL // LANES
    past = CONV_W - 1

    @pl.when(t == 0)
    def _():
        u_ref[:, 0:HALO, :] = jnp.zeros((n_chunks, HALO, LANES), F32)

    x = h_ref[...]
    xn = _rms(x, gin_ref[...]).astype(BF16)
    a = _dot(xn, w1_ref[...]) + b1_ref[...]
    glu = a[:, :D_MODEL] * _sigmoid(a[:, D_MODEL:])
    for c in range(n_chunks):
        u_ref[c, HALO:HALO + tile, :] = glu[:, c * LANES:(c + 1) * LANES]

    def taps(c, r0):
        lanes = slice(c * LANES, (c + 1) * LANES)
        acc = jnp.zeros((CONV_ROWS, LANES), F32)
        for j in range(CONV_W):
            acc = acc + wdw_ref[j:j + 1, lanes] * u_ref[c, pl.ds(r0 + (HALO - past) + j, CONV_ROWS), :]
        c_ref[pl.ds(r0, CONV_ROWS), lanes] = acc

    alongside()
    for c in range(n_chunks):
        for i in range(tile // CONV_ROWS):
            taps(c, i * CONV_ROWS)

    cv = c_ref[...] + bdw_ref[...]
    cn = _rms(cv, gmid_ref[...])
    act = (cn * _sigmoid(cn)).astype(BF16)
    o_ref[...] = x + _dot(act, w2_ref[...]) + b2_ref[...]

    @pl.when(t == n_tiles - 1)
    def _():
        for c in range(n_chunks):
            st_ref[:, c * LANES:(c + 1) * LANES] = u_ref[c, HALO + tile - past:HALO + tile, :]

    u_ref[:, 0:HALO, :] = u_ref[:, tile:tile + HALO, :]


def _conv_s_kernel(h_ref, st_ref, gin_ref, w1_ref, b1_ref, wdw_ref, bdw_ref, gmid_ref, w2_ref,
                   b2_ref, o_ref, nst_ref, *, bb, t_new):
    past = CONV_W - 1
    x = h_ref[...].reshape(t_new * bb, D_MODEL)
    xn = _rms(x, gin_ref[...]).astype(BF16)
    a = _dot(xn, w1_ref[...]) + b1_ref[...]
    glu = a[:, :D_MODEL] * _sigmoid(a[:, D_MODEL:])

    def u(i):
        return st_ref[i] if i < past else glu[(i - past) * bb:(i - past + 1) * bb]

    outs = []
    for t in range(t_new):
        acc = wdw_ref[0:1, :] * u(t)
        for j in range(1, CONV_W):
            acc = acc + wdw_ref[j:j + 1, :] * u(t + j)
        outs.append(acc)
    for i in range(past):
        nst_ref[i] = u(i + t_new)
    cv = jnp.concatenate(outs, axis=0) + bdw_ref[...]
    cn = _rms(cv, gmid_ref[...])
    act = (cn * _sigmoid(cn)).astype(BF16)
    y = x + _dot(act, w2_ref[...]) + b2_ref[...]
    o_ref[...] = y.reshape(t_new, bb, D_MODEL)


def _conv_sample(h, states, layer, gin, w1, b1, wdw, bdw, gmid, w2, b2, *, bb):
    t_new, n_b, d = h.shape
    past = CONV_W - 1
    tok = pl.BlockSpec((t_new, bb, d), lambda i: (0, i, 0))
    st_in = pl.BlockSpec((None, past, bb, d), lambda i: (layer, 0, i, 0))
    st_out = pl.BlockSpec((past, bb, d), lambda i: (0, i, 0))
    kern = functools.partial(_conv_s_kernel, bb=bb, t_new=t_new)
    return pl.pallas_call(
        kern,
        out_shape=(jax.ShapeDtypeStruct(h.shape, F32), jax.ShapeDtypeStruct((past, n_b, d), F32)),
        grid=(n_b // bb,),
        in_specs=[tok, st_in, _resident((1, d)), _resident((d, 2 * d)), _resident((1, 2 * d)),
                  _resident((CONV_W, d)), _resident((1, d)), _resident((1, d)),
                  _resident((d, d)), _resident((1, d))],
        out_specs=(tok, st_out),
        compiler_params=_params(1),
        name="conv_sample",
    )(h, states, gin, w1, b1, wdw, bdw, gmid, w2, b2)


def _head_rms(xh, g2, lo):
    sq = xh * xh
    s_lo = jnp.sum(jnp.where(lo, sq, 0.0), axis=-1, keepdims=True)
    s_hi = jnp.sum(jnp.where(lo, 0.0, sq), axis=-1, keepdims=True)
    inv = 1.0 / HEAD_DIM
    r = jnp.where(lo, lax.rsqrt(s_lo * inv + EPS), lax.rsqrt(s_hi * inv + EPS))
    return xh * r * g2


def _lo_mask():
    return lax.broadcasted_iota(jnp.int32, (1, V_DIM), 1) < HEAD_DIM


def _kv_body(x, g_ref, w_ref, kg_ref, k_ref, v_ref, *attn_copies):
    tm = x.shape[0]
    xn = _rms(x, g_ref[...]).astype(BF16)
    kv = _dot(xn, w_ref[...])
    lo = _lo_mask()
    for h in range(N_HEADS):
        lanes = slice(h * V_DIM, (h + 1) * V_DIM)
        kh = _head_rms(kv[:, lanes], kg_ref[...], lo)
        vh = kv[:, D_MODEL + h * V_DIM:D_MODEL + (h + 1) * V_DIM]
        if attn_copies:
            kb_ref, vt_ref = attn_copies
            k_ref[lanes, :] = kh.T
            kb_ref[h] = kh.astype(BF16)
            v_ref[pl.ds(h, tm, stride=N_HEADS), :] = vh
            vt_ref[h] = jnp.concatenate([vh.T.astype(BF16), jnp.ones((SUM_ROWS, tm), BF16)], axis=0)
        else:
            k_ref[:, lanes] = kh
            v_ref[:, lanes] = vh


def _kv_outputs(n, tm, batch, seq):
    d = D_MODEL
    tok = pl.BlockSpec((tm, d), lambda i: (i, 0))
    if batch is None:
        return [jax.ShapeDtypeStruct((n, d), F32)] * 2, [tok, tok]
    per = seq // tm
    shapes = [jax.ShapeDtypeStruct((batch, d, seq), F32),
              jax.ShapeDtypeStruct((n * N_HEADS, V_DIM), F32),
              jax.ShapeDtypeStruct((batch, N_HEADS, seq, V_DIM), BF16),
              jax.ShapeDtypeStruct((batch, N_HEADS, V_DIM + SUM_ROWS, seq), BF16)]
    specs = [pl.BlockSpec((None, d, tm), lambda i: (i // per, 0, i % per)),
             pl.BlockSpec((tm * N_HEADS, V_DIM), lambda i: (i, 0)),
             pl.BlockSpec((None, N_HEADS, tm, V_DIM), lambda i: (i // per, 0, i % per, 0)),
             pl.BlockSpec((None, N_HEADS, V_DIM + SUM_ROWS, tm), lambda i: (i // per, 0, 0, i % per))]
    return shapes, specs


def _q_body(x, g_ref, w_ref, qg_ref, q_ref, *, feature_major):
    xn = _rms(x, g_ref[...]).astype(BF16)
    q = _dot(xn, w_ref[...])
    lo = _lo_mask()
    scale = HEAD_DIM ** -0.5
    for h in range(N_HEADS):
        lanes = slice(h * V_DIM, (h + 1) * V_DIM)
        qh = _head_rms(q[:, lanes], qg_ref[...], lo) * scale
        if feature_major:
            q_ref[h] = (qh * LOG2E).T.astype(BF16)
        else:
            q_ref[:, lanes] = qh


def _q_outputs(n, tm, batch, seq):
    if batch is None:
        return jax.ShapeDtypeStruct((n, D_MODEL), F32), pl.BlockSpec((tm, D_MODEL), lambda i: (i, 0))
    per = seq // tm
    return (jax.ShapeDtypeStruct((batch, N_HEADS, V_DIM, seq), BF16),
            pl.BlockSpec((None, N_HEADS, V_DIM, tm), lambda i: (i // per, 0, 0, i % per)))


def _bucket(n):
    nf = jnp.maximum(n, 1).astype(F32)
    large = MAX_EXACT + (jnp.log(nf / MAX_EXACT) / math.log(MAX_DISTANCE / MAX_EXACT)
                         * (N_BUCKETS - MAX_EXACT)).astype(jnp.int32)
    large = jnp.minimum(large, N_BUCKETS - 1)
    return jnp.where(n < MAX_EXACT, n, large)


def _lambda(lq1_ref, lk1_ref, lq2_ref, lk2_ref, lam_init):
    s1 = jnp.sum(lq1_ref[...] * lk1_ref[...], axis=-1, keepdims=True)
    s2 = jnp.sum(lq2_ref[...] * lk2_ref[...], axis=-1, keepdims=True)
    return jnp.exp(s1) - jnp.exp(s2) + lam_init


LOGITS_AHEAD = 2
N_ATTN_IN = 8


def _attn_p_body(first, qi, rb_ref, lq1_ref, lk1_ref, lq2_ref, lk2_ref, qt_ref, k_ref, vt_ref, o_ref,
                 bias_ref, q2_ref, m_ref, acc_ref, *, blk, lam_init, alongside):
    @pl.when(first)
    def _():
        kr = lax.broadcasted_iota(jnp.int32, (blk, blk), 0)
        qc = lax.broadcasted_iota(jnp.int32, (blk, blk), 1)
        for which in range(2):
            dist = qc - kr + (blk if which == 0 else 0)
            bucket = _bucket(jnp.maximum(dist, 0))
            for h in range(N_HEADS):
                tile = jnp.zeros((blk, blk), F32)
                for b in range(N_BUCKETS - 1):
                    tile = jnp.where(bucket == b,
                                     (rb_ref[b, h] - rb_ref[N_BUCKETS - 1, h]) * LOG2E, tile)
                if which == 1:
                    tile = jnp.where(dist >= 0, tile, MASKED)
                bias_ref[which, h] = tile

    lam = _lambda(lq1_ref, lk1_ref, lq2_ref, lk2_ref, lam_init)
    row = lax.broadcasted_iota(jnp.int32, (V_DIM, 1), 0)
    zero = jnp.zeros((), BF16)

    for h in range(N_HEADS):
        qt = qt_ref[h]
        q2_ref[h] = jnp.concatenate([jnp.where(row < HEAD_DIM, qt, zero),
                                     jnp.where(row < HEAD_DIM, zero, qt)], axis=1)
    m_ref[...] = jnp.full(m_ref.shape, MASKED, F32)
    acc_ref[...] = jnp.zeros(acc_ref.shape, F32)

    def logits(h, start, which):
        s = _dot(k_ref[h, pl.ds(start, blk), :], q2_ref[h])
        if which is not None:
            bias = bias_ref[which, h]
            s = s + jnp.concatenate([bias, bias], axis=1)
        return s

    def update(h, start, s):
        m_old = m_ref[h]
        m_new = jnp.maximum(m_old, jnp.max(s, axis=0, keepdims=True))
        alpha = jnp.exp2(m_old - m_new)
        p = jnp.exp2(s - m_new).astype(BF16)
        acc_ref[h] = alpha * acc_ref[h] + _dot(vt_ref[h, :, pl.ds(start, blk)], p)
        m_ref[h] = m_new

    def key_blocks(*blocks):
        units = [(h, pl.multiple_of(j * blk, blk), which) for j, which in blocks for h in range(N_HEADS)]
        pending = [logits(*u) for u in units[:LOGITS_AHEAD]]
        for n, (h, start, _) in enumerate(units):
            if n + LOGITS_AHEAD < len(units):
                pending.append(logits(*units[n + LOGITS_AHEAD]))
            update(h, start, pending.pop(0))

    n_far = jnp.maximum(qi - 1, 0)

    def far_pair(i, c):
        key_blocks((2 * i, None), (2 * i + 1, None))
        return c

    lax.fori_loop(0, lax.shift_right_logical(n_far, 1), far_pair, 0)

    @pl.when(n_far & 1 == 1)
    def _():
        key_blocks((n_far - 1, None))

    @pl.when(qi > 0)
    def _():
        alongside()
        key_blocks((qi - 1, 0), (qi, 1))

    @pl.when(qi == 0)
    def _():
        alongside()
        key_blocks((qi, 1))

    for h in range(N_HEADS):
        w = acc_ref[h, :V_DIM, :] / acc_ref[h, V_DIM:V_DIM + 1, :]
        o_ref[h] = (w[:, :blk] - lam * w[:, blk:]).T


N_DECODE_IN = 8
DECODE_PARTS = 2


def _decode_init(first, inputs, scratch, *, page, t_new):
    rb_ref = inputs[0]
    bp_ref, bn_ref = scratch
    rows = 2 * N_HEADS * t_new
    half = N_HEADS * t_new

    @pl.when(first)
    def _():
        def tile(shape, dist):
            r = lax.broadcasted_iota(jnp.int32, shape, 0)
            head = (r % half) // t_new
            bucket = _bucket(jnp.maximum(dist, 0))
            out = jnp.zeros(shape, F32)
            for h in range(N_HEADS):
                for b in range(N_BUCKETS - 1):
                    out = jnp.where((head == h) & (bucket == b),
                                    rb_ref[b, h] - rb_ref[N_BUCKETS - 1, h], out)
            return jnp.where(dist >= 0, out, MASKED)

        r = lax.broadcasted_iota(jnp.int32, (rows, page), 0)
        c = lax.broadcasted_iota(jnp.int32, (rows, page), 1)
        bp_ref[...] = tile((rows, page), page + r % t_new - c)
        r = lax.broadcasted_iota(jnp.int32, (rows, t_new), 0)
        c = lax.broadcasted_iota(jnp.int32, (rows, t_new), 1)
        bn_ref[...] = tile((rows, t_new), r % t_new - c)


def _decode_main(inputs, o_ref, scratch, *, n_pages, page, t_new, lam_init):
    _, lq1_ref, lk1_ref, lq2_ref, lk2_ref, q_ref, kn_ref, vn_ref = inputs[:N_DECODE_IN]
    k_refs = inputs[N_DECODE_IN:N_DECODE_IN + n_pages]
    v_refs = inputs[N_DECODE_IN + n_pages:]
    bp_ref, bn_ref = scratch
    rows = 2 * N_HEADS * t_new
    half = N_HEADS * t_new
    lam = _lambda(lq1_ref, lk1_ref, lq2_ref, lk2_ref, lam_init)

    q = q_ref[...]
    r = lax.broadcasted_iota(jnp.int32, (rows, D_MODEL), 0)
    ln = lax.broadcasted_iota(jnp.int32, (rows, D_MODEL), 1)
    group = ((r % half) // t_new) * 2 + r // half
    q2 = jnp.where(ln // HEAD_DIM == group, jnp.concatenate([q] * (rows // t_new), axis=0), 0.0)
    q2 = q2.astype(BF16)

    past = n_pages * page

    def page_values(v):
        return jnp.concatenate([v[pl.ds(h, page, stride=N_HEADS), :].astype(BF16)
                                for h in range(N_HEADS)], axis=1)

    per = n_pages // DECODE_PARTS
    parts = [slice(i * per, (i + 1) * per) for i in range(DECODE_PARTS)]
    s_part, v_part = [], []
    for part in parts:
        kt = jnp.concatenate([k[...].astype(BF16) for k in k_refs[part]], axis=1)
        s_part.append(_dot(q2, kt))
        v_part.append(jnp.concatenate([page_values(v) for v in v_refs[part]], axis=0))
    s_past = jnp.concatenate(s_part, axis=1)
    s_past = jnp.concatenate([s_past[:, :past - page], s_past[:, past - page:] + bp_ref[...]], axis=1)
    s_new = _dot_nt(q2, kn_ref[...].astype(BF16)) + bn_ref[...]

    m = jnp.maximum(jnp.max(s_past, axis=-1, keepdims=True), jnp.max(s_new, axis=-1, keepdims=True))
    p_past = jnp.exp(s_past - m)
    p_new = jnp.exp(s_new - m)
    inv_l = 1.0 / (jnp.sum(p_past, axis=-1, keepdims=True) + jnp.sum(p_new, axis=-1, keepdims=True))
    w_past = p_past * inv_l
    w_new = p_new * inv_l
    a_past = (w_past[:half] - lam * w_past[half:]).astype(BF16)
    a_new = (w_new[:half] - lam * w_new[half:]).astype(BF16)

    keys = per * page
    full = _dot(a_new, vn_ref[...].astype(BF16))
    for i, v in enumerate(v_part):
        full = full + _dot(a_past[:, i * keys:(i + 1) * keys], v)
    for h in range(N_HEADS):
        o_ref[h] = full[h * t_new:(h + 1) * t_new, h * V_DIM:(h + 1) * V_DIM]


def _decode_specs(seq_of, out_of, n_pages, page, t_new):
    d = D_MODEL
    rows = 2 * N_HEADS * t_new
    vec = pl.BlockSpec((1, HEAD_DIM), lambda *a: (0, 0))
    tok = pl.BlockSpec((t_new, d), lambda *a: (seq_of(*a[:-1]), 0))

    def page_spec(i, shape):
        return pl.BlockSpec((None,) + shape, lambda *a: (a[-1][seq_of(*a[:-1]), i], 0, 0))

    in_specs = [pl.BlockSpec(memory_space=pltpu.SMEM), vec, vec, vec, vec, tok, tok, tok]
    in_specs += [page_spec(i, (d, page)) for i in range(n_pages)]
    in_specs += [page_spec(i, (page * N_HEADS, V_DIM)) for i in range(n_pages)]
    out_spec = pl.BlockSpec((N_HEADS, t_new, V_DIM), lambda *a: (0, out_of(*a[:-1]), 0))
    scratch = [pltpu.VMEM((rows, page), F32), pltpu.VMEM((rows, t_new), F32)]
    return in_specs, out_spec, scratch


def _decode_operands(q, k_new, v_new, cache_kt, cache_v, rel_bias, lam_vecs, n_pages):
    return (rel_bias, *lam_vecs, q, k_new, v_new, *([cache_kt] * n_pages), *([cache_v] * n_pages))


N_CONV_IN = 9


def _conv_decode_kernel(pt_ref, *refs, tile, n_tiles, n_pages, page, t_new, lam_init):
    del pt_ref
    n_dec = N_DECODE_IN + 2 * n_pages
    conv_in, dec_in = refs[:N_CONV_IN], refs[N_CONV_IN:N_CONV_IN + n_dec]
    o_ref, st_ref, od_ref, u_ref, c_ref, bp_ref, bn_ref = refs[N_CONV_IN + n_dec:]
    first = (pl.program_id(0) == 0) & (pl.program_id(1) == 0)
    _decode_init(first, dec_in, (bp_ref, bn_ref), page=page, t_new=t_new)
    decode = functools.partial(_decode_main, dec_in, od_ref, (bp_ref, bn_ref),
                               n_pages=n_pages, page=page, t_new=t_new, lam_init=lam_init)
    _conv_p_body(pl.program_id(1), *conv_in, o_ref, st_ref, u_ref, c_ref, tile=tile, n_tiles=n_tiles,
                 alongside=decode)


def _conv_prompt_decode(h, batch, seq, conv_params, decode_operands, page_table, seq_offset,
                        lam_init, *, tile, t_new):
    d = D_MODEL
    n_tiles = seq // tile
    n_steps = batch * n_tiles
    n_pages = page_table.shape[1]
    page = decode_operands[-1].shape[1] // N_HEADS
    tok = pl.BlockSpec((tile, d), lambda b, t, pt: (b * n_tiles + t, 0))
    conv_specs = [tok, _resident((1, d)), _resident((d, 2 * d)), _resident((1, 2 * d)),
                  _resident((CONV_W, d)), _resident((1, d)), _resident((1, d)),
                  _resident((d, d)), _resident((1, d))]
    dec_specs, dec_out, dec_scratch = _decode_specs(
        lambda b, t: seq_offset + b * n_tiles + t, lambda b, t: b * n_tiles + t, n_pages, page, t_new)
    kern = functools.partial(_conv_decode_kernel, tile=tile, n_tiles=n_tiles, n_pages=n_pages,
                             page=page, t_new=t_new, lam_init=lam_init)
    grid_spec = pltpu.PrefetchScalarGridSpec(
        num_scalar_prefetch=1, grid=(batch, n_tiles),
        in_specs=conv_specs + dec_specs,
        out_specs=(tok, pl.BlockSpec((None, CONV_W - 1, d), lambda b, t, pt: (b, 0, 0)), dec_out),
        scratch_shapes=[pltpu.VMEM((d // LANES, HALO + tile, LANES), F32),
                        pltpu.VMEM((tile, d), F32)] + dec_scratch)
    return pl.pallas_call(
        kern,
        out_shape=(jax.ShapeDtypeStruct((batch * seq, d), F32),
                   jax.ShapeDtypeStruct((batch, CONV_W - 1, d), F32),
                   jax.ShapeDtypeStruct((N_HEADS, n_steps * t_new, V_DIM), F32)),
        grid_spec=grid_spec,
        compiler_params=_params(2),
        name="conv_prompt_decode",
    )(page_table, h, *conv_params, *decode_operands)


def _attn_decode_kernel(pt_ref, *refs, blk, lam_init, n_pages, page, t_new, lam_init_decode):
    del pt_ref
    n_dec = N_DECODE_IN + 2 * n_pages
    attn_in, dec_in = refs[:N_ATTN_IN], refs[N_ATTN_IN:N_ATTN_IN + n_dec]
    o_ref, od_ref, bias_ref, q2_ref, m_ref, acc_ref, bp_ref, bn_ref = refs[N_ATTN_IN + n_dec:]
    first = (pl.program_id(0) == 0) & (pl.program_id(1) == 0)
    _decode_init(first, dec_in, (bp_ref, bn_ref), page=page, t_new=t_new)
    decode = functools.partial(_decode_main, dec_in, od_ref, (bp_ref, bn_ref),
                               n_pages=n_pages, page=page, t_new=t_new, lam_init=lam_init_decode)
    _attn_p_body(first, pl.program_id(1), *attn_in, o_ref, bias_ref, q2_ref, m_ref, acc_ref,
                 blk=blk, lam_init=lam_init, alongside=decode)


def _attn_prompt_decode(qt, k, vt, rel_bias, lam_vecs, lam_init, decode_operands, page_table,
                        seq_offset, lam_init_decode, *, blk, t_new):
    batch, _, _, seq = qt.shape
    n_blk = seq // blk
    n_pages = page_table.shape[1]
    page = decode_operands[-1].shape[1] // N_HEADS
    vec = _resident((1, HEAD_DIM))
    once = dict(pipeline_mode=pl.Buffered(1))
    attn_specs = [pl.BlockSpec(memory_space=pltpu.SMEM), vec, vec, vec, vec,
                  pl.BlockSpec((None, N_HEADS, V_DIM, blk), lambda b, i, pt: (b, 0, 0, i)),
                  pl.BlockSpec((None, N_HEADS, seq, V_DIM), lambda b, i, pt: (b, 0, 0, 0), **once),
                  pl.BlockSpec((None, N_HEADS, V_DIM + SUM_ROWS, seq), lambda b, i, pt: (b, 0, 0, 0), **once)]
    dec_specs, dec_out, dec_scratch = _decode_specs(
        lambda b, i: seq_offset + b * n_blk + i, lambda b, i: b * n_blk + i, n_pages, page, t_new)
    kern = functools.partial(_attn_decode_kernel, blk=blk, lam_init=lam_init, n_pages=n_pages,
                             page=page, t_new=t_new, lam_init_decode=lam_init_decode)
    grid_spec = pltpu.PrefetchScalarGridSpec(
        num_scalar_prefetch=1, grid=(batch, n_blk),
        in_specs=attn_specs + dec_specs,
        out_specs=(pl.BlockSpec((None, N_HEADS, blk, V_DIM), lambda b, i, pt: (b, 0, i, 0)), dec_out),
        scratch_shapes=[pltpu.VMEM((2, N_HEADS, blk, blk), F32),
                        pltpu.VMEM((N_HEADS, V_DIM, 2 * blk), BF16),
                        pltpu.VMEM((N_HEADS, 1, 2 * blk), F32),
                        pltpu.VMEM((N_HEADS, V_DIM + SUM_ROWS, 2 * blk), F32)]
        + dec_scratch)
    return pl.pallas_call(
        kern,
        out_shape=(jax.ShapeDtypeStruct((batch, N_HEADS, seq, V_DIM), F32),
                   jax.ShapeDtypeStruct((N_HEADS, batch * n_blk * t_new, V_DIM), F32)),
        grid_spec=grid_spec,
        compiler_params=_params(2),
        name="attn_prompt_decode",
    )(page_table, rel_bias, *lam_vecs, qt, k, vt, *decode_operands)


def _lambda_init(layer_idx):
    return 0.8 - 0.6 * math.exp(-0.3 * layer_idx)


def kernel(x_prompt, x_sample, state_conv, cache_k, cache_v, page_table, rel_bias, conv_norm, w_pw1, b_pw1, w_dw, b_dw, conv_mid_norm, w_pw2, b_pw2, kv_norm, w_kv, k_norm, attn_norm, w_q, q_norm, lambda_q1, lambda_k1, lambda_q2, lambda_k2, sub_norm, w_o, ffn_norm, w_gate, w_up, w_down):
    batch, seq, d = x_prompt.shape
    n_b, t_new, _ = x_sample.shape
    n_a = state_conv.shape[0]
    depth = ffn_norm.shape[0]
    n_pool, page = cache_k.shape[:2]

    row = lambda a: a.reshape(1, -1)
    two = lambda a: jnp.concatenate([a, a]).reshape(1, V_DIM)
    bf = lambda a: a.astype(BF16)

    hp = x_prompt.reshape(batch * seq, d)
    ckt = jnp.transpose(cache_k, (0, 2, 3, 4, 1)).reshape(n_pool, d, page)
    cv = cache_v.reshape(n_pool, page * N_HEADS, V_DIM)
    state_t = jnp.transpose(state_conv, (0, 2, 1, 3))
    hs_t = jnp.transpose(x_sample, (1, 0, 2))

    wg_all, wu_all, wd_all = bf(w_gate), bf(w_up), bf(w_down)

    def ffn_weights(l):
        return row(ffn_norm[l]), wg_all, wu_all, wd_all, l

    def conv_weights(l):
        return (row(conv_norm[l]), bf(w_pw1[l]), row(b_pw1[l]), w_dw[l], row(b_dw[l]),
                row(conv_mid_norm[l]), bf(w_pw2[l]), row(b_pw2[l]))

    n_attn = depth - n_a
    n_pages = page_table.shape[1]
    lam_vecs = [(row(lambda_q1[j]), row(lambda_k1[j]), row(lambda_q2[j]), row(lambda_k2[j]))
                for j in range(n_attn)]
    lam_inits = [_lambda_init(n_a + j) for j in range(n_attn)]
    kv_weights = (row(kv_norm), bf(w_kv), two(k_norm))

    def q_weights(j):
        return row(attn_norm[j]), bf(w_q[j]), two(q_norm[j])

    def attn_out_weights(o, j):
        return o, row(sub_norm[j]), bf(w_o[j]), 1.0 - lam_inits[j]

    assert n_a >= 1 and n_attn == 2, "kernel schedule below is laid out for conv layers then two attention layers"
    conv_tile = attn_blk = 256
    steps = batch * (seq // conv_tile)
    assert n_a * steps == n_b and n_attn * steps == n_b, "one sample sequence's decode per prompt grid step"
    prompt = dict(tm=512, batch=batch, seq=seq)
    sample = dict(tm=512)

    conv_s = []
    for l in range(n_a):
        hs_t, ss = _conv_sample(hs_t, state_t, l, *conv_weights(l), bb=16)
        conv_s.append(ss)
        if l < n_a - 1:
            hs_t, = _layer_tail(hs_t.reshape(t_new * n_b, d), ffn_weights(l), **sample)
            hs_t = hs_t.reshape(t_new, n_b, d)
    hs = jnp.transpose(hs_t, (1, 0, 2)).reshape(n_b * t_new, d)
    hs, ksm, vsm, qs = _layer_tail(hs, ffn_weights(n_a - 1), kv=kv_weights, q=q_weights(0), **sample)
    decode = _decode_operands(qs, ksm, vsm, ckt, cv, rel_bias, lam_vecs[0], n_pages)

    conv_p, os_parts = [], []
    for l in range(n_a):
        hp, sp, od = _conv_prompt_decode(hp, batch, seq, conv_weights(l), decode, page_table,
                                         l * steps, lam_inits[0], tile=conv_tile, t_new=t_new)
        conv_p.append(sp)
        os_parts.append(od)
        if l < n_a - 1:
            hp, = _layer_tail(hp, ffn_weights(l), **prompt)
    hp, kpt, vp, kpb, vpt, qp = _layer_tail(hp, ffn_weights(n_a - 1), kv=kv_weights, q=q_weights(0),
                                            **prompt)
    os_ = jnp.concatenate(os_parts, axis=1)

    hs, qs = _layer_tail(hs, ffn_weights(n_a), attn=attn_out_weights(os_[None], 0), q=q_weights(1),
                         **sample)
    decode = _decode_operands(qs, ksm, vsm, ckt, cv, rel_bias, lam_vecs[1], n_pages)
    os_parts = []
    for j in range(n_attn):
        op, od = _attn_prompt_decode(qp, kpb, vpt, rel_bias, lam_vecs[j], lam_inits[j], decode,
                                     page_table, j * steps, lam_inits[1], blk=attn_blk, t_new=t_new)
        os_parts.append(od)
        if j + 1 < n_attn:
            hp, qp = _layer_tail(hp, ffn_weights(n_a + j), attn=attn_out_weights(op, j),
                                 q=q_weights(j + 1), **prompt)
        else:
            hp, = _layer_tail(hp, ffn_weights(n_a + j), attn=attn_out_weights(op, j), **prompt)
    hs, = _layer_tail(hs, ffn_weights(n_a + 1), attn=attn_out_weights(jnp.concatenate(os_parts, axis=1)[None], 1),
                      **sample)

    k_prompt = jnp.transpose(kpt.reshape(batch, N_HEADS, 2, HEAD_DIM, seq), (0, 4, 1, 2, 3))
    return (hp.reshape(batch, seq, d), hs.reshape(n_b, t_new, d),
            jnp.stack(conv_p), jnp.transpose(jnp.stack(conv_s), (0, 2, 1, 3)),
            k_prompt, vp.reshape(batch, seq, N_HEADS, V_DIM),
            ksm.reshape(n_b, t_new, N_HEADS, 2, HEAD_DIM), vsm.reshape(n_b, t_new, N_HEADS, V_DIM))
```

```python
import functools
import math

import jax
import jax.numpy as jnp
from jax import lax
from jax.experimental import pallas as pl
from jax.experimental.pallas import tpu as pltpu

D_MODEL = 1024
N_HEADS = 8
HEAD_DIM = 64
V_DIM = 2 * HEAD_DIM
CONV_W = 31
N_BUCKETS = 32
MAX_EXACT = N_BUCKETS // 2
MAX_DISTANCE = 128
EPS = 1e-6
MASKED = -1e30
LOG2E = math.log2(math.e)

LANES = 128
VMEM_LIMIT = 56 * 1024 * 1024

F32 = jnp.float32
BF16 = jnp.bfloat16


def _params(n_grid_dims):
    return pltpu.CompilerParams(
        dimension_semantics=("arbitrary",) * n_grid_dims,
        vmem_limit_bytes=VMEM_LIMIT)


def _resident(shape):
    zeros = (0,) * len(shape)
    return pl.BlockSpec(shape, lambda *_: zeros, pipeline_mode=pl.Buffered(1))


def _rms(x, g):
    return x * lax.rsqrt(jnp.mean(x * x, axis=-1, keepdims=True) + EPS) * g


def _sigmoid(x):
    return 1.0 / (1.0 + jnp.exp(-x))


def _dot(a, b):
    return jnp.dot(a, b, preferred_element_type=F32)


def _dot_nt(a, b):
    return lax.dot_general(a, b, (((1,), (1,)), ((), ())), preferred_element_type=F32)


def _layer_tail_kernel(*refs, scale, has_kv, has_q, feature_major):
    refs = list(refs)
    take = lambda n: [refs.pop(0) for _ in range(n)]
    if scale is not None:
        o_ref, = take(1)
    h_ref, = take(1)
    if scale is not None:
        gs_ref, wo_ref = take(2)
    g_ref, wg_ref, wu_ref, wd_ref = take(4)
    kv_in = take(3) if has_kv else None
    q_in = take(3) if has_q else None
    out_ref, = take(1)
    kv_out = take(4 if feature_major else 2) if has_kv else None
    q_out = take(1) if has_q else None

    x = h_ref[...]
    if scale is not None:
        heads = [(_rms(o_ref[h], gs_ref[...]) * scale).astype(BF16) for h in range(N_HEADS)]
        x = x + _dot(jnp.concatenate(heads, axis=1), wo_ref[...])
    xn = _rms(x, g_ref[...]).astype(BF16)
    gate = _dot(xn, wg_ref[...])
    up = _dot(xn, wu_ref[...])
    act = (gate * _sigmoid(gate) * up).astype(BF16)
    y = x + _dot(act, wd_ref[...])
    out_ref[...] = y
    if has_kv:
        _kv_body(y, *kv_in, *kv_out)
    if has_q:
        _q_body(y, *q_in, *q_out, feature_major=feature_major)


def _resident_layer(shape, layer):
    index = (layer,) + (0,) * len(shape)
    return pl.BlockSpec((None,) + shape, lambda *_: index, pipeline_mode=pl.Buffered(1))


def _layer_tail(h, ffn, *, tm, attn=None, kv=None, q=None, batch=None, seq=None):
    n, d = h.shape
    g, wg, wu, wd, layer = ffn
    f = wg.shape[2]
    tok = pl.BlockSpec((tm, d), lambda i: (i, 0))
    operands, in_specs = [], []
    if attn is not None:
        o, g_sub, wo, scale = attn
        per = o.shape[2] // tm
        operands.append(o)
        in_specs.append(pl.BlockSpec((None, N_HEADS, tm, V_DIM), lambda i: (i // per, 0, i % per, 0)))
    operands.append(h)
    in_specs.append(tok)
    if attn is not None:
        operands += [g_sub, wo]
        in_specs += [_resident((1, V_DIM)), _resident((d, d))]
    operands += [g, wg, wu, wd]
    in_specs += [_resident((1, d)), _resident_layer((d, f), layer), _resident_layer((d, f), layer),
                 _resident_layer((f, d), layer)]
    out_shape, out_specs = [jax.ShapeDtypeStruct((n, d), F32)], [tok]
    if kv is not None:
        operands += list(kv)
        in_specs += [_resident((1, d)), _resident((d, 2 * d)), _resident((1, V_DIM))]
        shapes, specs = _kv_outputs(n, tm, batch, seq)
        out_shape += shapes
        out_specs += specs
    if q is not None:
        operands += list(q)
        in_specs += [_resident((1, d)), _resident((d, d)), _resident((1, V_DIM))]
        shape, spec = _q_outputs(n, tm, batch, seq)
        out_shape.append(shape)
        out_specs.append(spec)
    kern = functools.partial(_layer_tail_kernel, scale=None if attn is None else attn[3],
                             has_kv=kv is not None, has_q=q is not None,
                             feature_major=batch is not None)
    return pl.pallas_call(
        kern,
        out_shape=tuple(out_shape),
        grid=(n // tm,),
        in_specs=in_specs,
        out_specs=tuple(out_specs),
        compiler_params=_params(1),
        name="layer_tail",
    )(*operands)


SUM_ROWS = 16
HALO = 32
CONV_ROWS = 128


def _conv_p_body(t, h_ref, gin_ref, w1_ref, b1_ref, wdw_ref, bdw_ref, gmid_ref, w2_ref, b2_ref,
                 o_ref, st_ref, u_ref, c_ref, *, tile, n_tiles, alongside):
    n_chunks = D_MODEL // LANES
    past = CONV_W - 1

    @pl.when(t == 0)
    def _():
        u_ref[:, 0:HALO, :] = jnp.zeros((n_chunks, HALO, LANES), F32)

    x = h_ref[...]
    xn = _rms(x, gin_ref[...]).astype(BF16)
    a = _dot(xn, w1_ref[...]) + b1_ref[...]
    glu = a[:, :D_MODEL] * _sigmoid(a[:, D_MODEL:])
    for c in range(n_chunks):
        u_ref[c, HALO:HALO + tile, :] = glu[:, c * LANES:(c + 1) * LANES]

    def taps(c, r0):
        lanes = slice(c * LANES, (c + 1) * LANES)
        acc = jnp.zeros((CONV_ROWS, LANES), F32)
        for j in range(CONV_W):
            acc = acc + wdw_ref[j:j + 1, lanes] * u_ref[c, pl.ds(r0 + (HALO - past) + j, CONV_ROWS), :]
        c_ref[pl.ds(r0, CONV_ROWS), lanes] = acc

    alongside()
    for c in range(n_chunks):
        for i in range(tile // CONV_ROWS):
            taps(c, i * CONV_ROWS)

    cv = c_ref[...] + bdw_ref[...]
    cn = _rms(cv, gmid_ref[...])
    act = (cn * _sigmoid(cn)).astype(BF16)
    o_ref[...] = x + _dot(act, w2_ref[...]) + b2_ref[...]

    @pl.when(t == n_tiles - 1)
    def _():
        for c in range(n_chunks):
            st_ref[:, c * LANES:(c + 1) * LANES] = u_ref[c, HALO + tile - past:HALO + tile, :]

    u_ref[:, 0:HALO, :] = u_ref[:, tile:tile + HALO, :]


def _conv_s_kernel(h_ref, st_ref, gin_ref, w1_ref, b1_ref, wdw_ref, bdw_ref, gmid_ref, w2_ref,
                   b2_ref, o_ref, nst_ref, *, bb, t_new):
    past = CONV_W - 1
    x = h_ref[...].reshape(t_new * bb, D_MODEL)
    xn = _rms(x, gin_ref[...]).astype(BF16)
    a = _dot(xn, w1_ref[...]) + b1_ref[...]
    glu = a[:, :D_MODEL] * _sigmoid(a[:, D_MODEL:])

    def u(i):
        return st_ref[i] if i < past else glu[(i - past) * bb:(i - past + 1) * bb]

    outs = []
    for t in range(t_new):
        acc = wdw_ref[0:1, :] * u(t)
        for j in range(1, CONV_W):
            acc = acc + wdw_ref[j:j + 1, :] * u(t + j)
        outs.append(acc)
    for i in range(past):
        nst_ref[i] = u(i + t_new)
    cv = jnp.concatenate(outs, axis=0) + bdw_ref[...]
    cn = _rms(cv, gmid_ref[...])
    act = (cn * _sigmoid(cn)).astype(BF16)
    y = x + _dot(act, w2_ref[...]) + b2_ref[...]
    o_ref[...] = y.reshape(t_new, bb, D_MODEL)


def _conv_sample(h, states, layer, gin, w1, b1, wdw, bdw, gmid, w2, b2, *, bb):
    t_new, n_b, d = h.shape
    past = CONV_W - 1
    tok = pl.BlockSpec((t_new, bb, d), lambda i: (0, i, 0))
    st_in = pl.BlockSpec((None, past, bb, d), lambda i: (layer, 0, i, 0))
    st_out = pl.BlockSpec((past, bb, d), lambda i: (0, i, 0))
    kern = functools.partial(_conv_s_kernel, bb=bb, t_new=t_new)
    return pl.pallas_call(
        kern,
        out_shape=(jax.ShapeDtypeStruct(h.shape, F32), jax.ShapeDtypeStruct((past, n_b, d), F32)),
        grid=(n_b // bb,),
        in_specs=[tok, st_in, _resident((1, d)), _resident((d, 2 * d)), _resident((1, 2 * d)),
                  _resident((CONV_W, d)), _resident((1, d)), _resident((1, d)),
                  _resident((d, d)), _resident((1, d))],
        out_specs=(tok, st_out),
        compiler_params=_params(1),
        name="conv_sample",
    )(h, states, gin, w1, b1, wdw, bdw, gmid, w2, b2)


def _head_rms(xh, g2, lo):
    sq = xh * xh
    s_lo = jnp.sum(jnp.where(lo, sq, 0.0), axis=-1, keepdims=True)
    s_hi = jnp.sum(jnp.where(lo, 0.0, sq), axis=-1, keepdims=True)
    inv = 1.0 / HEAD_DIM
    r = jnp.where(lo, lax.rsqrt(s_lo * inv + EPS), lax.rsqrt(s_hi * inv + EPS))
    return xh * r * g2


def _lo_mask():
    return lax.broadcasted_iota(jnp.int32, (1, V_DIM), 1) < HEAD_DIM


def _kv_body(x, g_ref, w_ref, kg_ref, k_ref, v_ref, *attn_copies):
    tm = x.shape[0]
    xn = _rms(x, g_ref[...]).astype(BF16)
    kv = _dot(xn, w_ref[...])
    lo = _lo_mask()
    for h in range(N_HEADS):
        lanes = slice(h * V_DIM, (h + 1) * V_DIM)
        kh = _head_rms(kv[:, lanes], kg_ref[...], lo)
        vh = kv[:, D_MODEL + h * V_DIM:D_MODEL + (h + 1) * V_DIM]
        if attn_copies:
            kb_ref, vt_ref = attn_copies
            k_ref[lanes, :] = kh.T
            kb_ref[h] = kh.astype(BF16)
            v_ref[pl.ds(h, tm, stride=N_HEADS), :] = vh
            vt_ref[h] = jnp.concatenate([vh.T.astype(BF16), jnp.ones((SUM_ROWS, tm), BF16)], axis=0)
        else:
            k_ref[:, lanes] = kh
            v_ref[:, lanes] = vh


def _kv_outputs(n, tm, batch, seq):
    d = D_MODEL
    tok = pl.BlockSpec((tm, d), lambda i: (i, 0))
    if batch is None:
        return [jax.ShapeDtypeStruct((n, d), F32)] * 2, [tok, tok]
    per = seq // tm
    shapes = [jax.ShapeDtypeStruct((batch, d, seq), F32),
              jax.ShapeDtypeStruct((n * N_HEADS, V_DIM), F32),
              jax.ShapeDtypeStruct((batch, N_HEADS, seq, V_DIM), BF16),
              jax.ShapeDtypeStruct((batch, N_HEADS, V_DIM + SUM_ROWS, seq), BF16)]
    specs = [pl.BlockSpec((None, d, tm), lambda i: (i // per, 0, i % per)),
             pl.BlockSpec((tm * N_HEADS, V_DIM), lambda i: (i, 0)),
             pl.BlockSpec((None, N_HEADS, tm, V_DIM), lambda i: (i // per, 0, i % per, 0)),
             pl.BlockSpec((None, N_HEADS, V_DIM + SUM_ROWS, tm), lambda i: (i // per, 0, 0, i % per))]
    return shapes, specs


def _q_body(x, g_ref, w_ref, qg_ref, q_ref, *, feature_major):
    xn = _rms(x, g_ref[...]).astype(BF16)
    q = _dot(xn, w_ref[...])
    lo = _lo_mask()
    scale = HEAD_DIM ** -0.5
    for h in range(N_HEADS):
        lanes = slice(h * V_DIM, (h + 1) * V_DIM)
        qh = _head_rms(q[:, lanes], qg_ref[...], lo) * scale
        if feature_major:
            q_ref[h] = (qh * LOG2E).T.astype(BF16)
        else:
            q_ref[:, lanes] = qh


def _q_outputs(n, tm, batch, seq):
    if batch is None:
        return jax.ShapeDtypeStruct((n, D_MODEL), F32), pl.BlockSpec((tm, D_MODEL), lambda i: (i, 0))
    per = seq // tm
    return (jax.ShapeDtypeStruct((batch, N_HEADS, V_DIM, seq), BF16),
            pl.BlockSpec((None, N_HEADS, V_DIM, tm), lambda i: (i // per, 0, 0, i % per)))


def _bucket(n):
    nf = jnp.maximum(n, 1).astype(F32)
    large = MAX_EXACT + (jnp.log(nf / MAX_EXACT) / math.log(MAX_DISTANCE / MAX_EXACT)
                         * (N_BUCKETS - MAX_EXACT)).astype(jnp.int32)
    large = jnp.minimum(large, N_BUCKETS - 1)
    return jnp.where(n < MAX_EXACT, n, large)


def _lambda(lq1_ref, lk1_ref, lq2_ref, lk2_ref, lam_init):
    s1 = jnp.sum(lq1_ref[...] * lk1_ref[...], axis=-1, keepdims=True)
    s2 = jnp.sum(lq2_ref[...] * lk2_ref[...], axis=-1, keepdims=True)
    return jnp.exp(s1) - jnp.exp(s2) + lam_init


LOGITS_AHEAD = 2
N_ATTN_IN = 8


def _attn_p_body(first, qi, rb_ref, lq1_ref, lk1_ref, lq2_ref, lk2_ref, qt_ref, k_ref, vt_ref, o_ref,
                 bias_ref, q2_ref, m_ref, acc_ref, *, blk, lam_init):
    @pl.when(first)
    def _():
        kr = lax.broadcasted_iota(jnp.int32, (blk, blk), 0)
        qc = lax.broadcasted_iota(jnp.int32, (blk, blk), 1)
        for which in range(2):
            dist = qc - kr + (blk if which == 0 else 0)
            bucket = _bucket(jnp.maximum(dist, 0))
            for h in range(N_HEADS):
                tile = jnp.zeros((blk, blk), F32)
                for b in range(N_BUCKETS - 1):
                    tile = jnp.where(bucket == b,
                                     (rb_ref[b, h] - rb_ref[N_BUCKETS - 1, h]) * LOG2E, tile)
                if which == 1:
                    tile = jnp.where(dist >= 0, tile, MASKED)
                bias_ref[which, h] = tile

    lam = _lambda(lq1_ref, lk1_ref, lq2_ref, lk2_ref, lam_init)
    row = lax.broadcasted_iota(jnp.int32, (V_DIM, 1), 0)
    zero = jnp.zeros((), BF16)

    for h in range(N_HEADS):
        qt = qt_ref[h]
        q2_ref[h] = jnp.concatenate([jnp.where(row < HEAD_DIM, qt, zero),
                                     jnp.where(row < HEAD_DIM, zero, qt)], axis=1)
    m_ref[...] = jnp.full(m_ref.shape, MASKED, F32)
    acc_ref[...] = jnp.zeros(acc_ref.shape, F32)

    def logits(h, start, which):
        s = _dot(k_ref[h, pl.ds(start, blk), :], q2_ref[h])
        if which is not None:
            bias = bias_ref[which, h]
            s = s + jnp.concatenate([bias, bias], axis=1)
        return s

    def update(h, start, s):
        m_old = m_ref[h]
        m_new = jnp.maximum(m_old, jnp.max(s, axis=0, keepdims=True))
        alpha = jnp.exp2(m_old - m_new)
        p = jnp.exp2(s - m_new).astype(BF16)
        acc_ref[h] = alpha * acc_ref[h] + _dot(vt_ref[h, :, pl.ds(start, blk)], p)
        m_ref[h] = m_new

    def key_blocks(*blocks):
        units = [(h, pl.multiple_of(j * blk, blk), which) for j, which in blocks for h in range(N_HEADS)]
        pending = [logits(*u) for u in units[:LOGITS_AHEAD]]
        for n, (h, start, _) in enumerate(units):
            if n + LOGITS_AHEAD < len(units):
                pending.append(logits(*units[n + LOGITS_AHEAD]))
            update(h, start, pending.pop(0))

    n_far = jnp.maximum(qi - 1, 0)

    def far_pair(i, c):
        key_blocks((2 * i, None), (2 * i + 1, None))
        return c

    lax.fori_loop(0, lax.shift_right_logical(n_far, 1), far_pair, 0)

    @pl.when(n_far & 1 == 1)
    def _():
        key_blocks((n_far - 1, None))

    @pl.when(qi > 0)
    def _():
        key_blocks((qi - 1, 0), (qi, 1))

    @pl.when(qi == 0)
    def _():
        key_blocks((qi, 1))

    for h in range(N_HEADS):
        w = acc_ref[h, :V_DIM, :] / acc_ref[h, V_DIM:V_DIM + 1, :]
        o_ref[h] = (w[:, :blk] - lam * w[:, blk:]).T


N_DECODE_IN = 8
DECODE_PARTS = 2


def _decode_init(first, inputs, scratch, *, page, t_new):
    rb_ref = inputs[0]
    bp_ref, bn_ref = scratch
    rows = 2 * N_HEADS * t_new
    half = N_HEADS * t_new

    @pl.when(first)
    def _():
        def tile(shape, dist):
            r = lax.broadcasted_iota(jnp.int32, shape, 0)
            head = (r % half) // t_new
            bucket = _bucket(jnp.maximum(dist, 0))
            out = jnp.zeros(shape, F32)
            for h in range(N_HEADS):
                for b in range(N_BUCKETS - 1):
                    out = jnp.where((head == h) & (bucket == b),
                                    (rb_ref[b, h] - rb_ref[N_BUCKETS - 1, h]) * LOG2E, out)
            return jnp.where(dist >= 0, out, MASKED)

        r = lax.broadcasted_iota(jnp.int32, (rows, page), 0)
        c = lax.broadcasted_iota(jnp.int32, (rows, page), 1)
        bp_ref[...] = tile((rows, page), page + r % t_new - c)
        r = lax.broadcasted_iota(jnp.int32, (rows, t_new), 0)
        c = lax.broadcasted_iota(jnp.int32, (rows, t_new), 1)
        bn_ref[...] = tile((rows, t_new), r % t_new - c)


def _decode_main(inputs, o_ref, scratch, *, n_pages, page, t_new, lam_init):
    _, lq1_ref, lk1_ref, lq2_ref, lk2_ref, q_ref, kn_ref, vn_ref = inputs[:N_DECODE_IN]
    k_refs = inputs[N_DECODE_IN:N_DECODE_IN + n_pages]
    v_refs = inputs[N_DECODE_IN + n_pages:]
    bp_ref, bn_ref = scratch
    rows = 2 * N_HEADS * t_new
    half = N_HEADS * t_new
    lam = _lambda(lq1_ref, lk1_ref, lq2_ref, lk2_ref, lam_init)

    q = q_ref[...] * LOG2E
    r = lax.broadcasted_iota(jnp.int32, (rows, D_MODEL), 0)
    ln = lax.broadcasted_iota(jnp.int32, (rows, D_MODEL), 1)
    group = ((r % half) // t_new) * 2 + r // half
    q2 = jnp.where(ln // HEAD_DIM == group, jnp.concatenate([q] * (rows // t_new), axis=0), 0.0)
    q2 = q2.astype(BF16)

    past = n_pages * page

    def page_values(v):
        return jnp.concatenate([v[pl.ds(h, page, stride=N_HEADS), :].astype(BF16)
                                for h in range(N_HEADS)], axis=1)

    per = n_pages // DECODE_PARTS
    parts = [slice(i * per, (i + 1) * per) for i in range(DECODE_PARTS)]
    s_part, v_part = [], []
    for part in parts:
        kt = jnp.concatenate([k[...].astype(BF16) for k in k_refs[part]], axis=1)
        s_part.append(_dot(q2, kt))
        v_part.append(jnp.concatenate([page_values(v) for v in v_refs[part]], axis=0))
    s_past = jnp.concatenate(s_part, axis=1)
    s_past = jnp.concatenate([s_past[:, :past - page], s_past[:, past - page:] + bp_ref[...]], axis=1)
    s_new = _dot_nt(q2, kn_ref[...].astype(BF16)) + bn_ref[...]

    m = jnp.maximum(jnp.max(s_past, axis=-1, keepdims=True), jnp.max(s_new, axis=-1, keepdims=True))
    p_past = jnp.exp2(s_past - m)
    p_new = jnp.exp2(s_new - m)
    inv_l = 1.0 / (jnp.sum(p_past, axis=-1, keepdims=True) + jnp.sum(p_new, axis=-1, keepdims=True))
    w_past = p_past * inv_l
    w_new = p_new * inv_l
    a_past = (w_past[:half] - lam * w_past[half:]).astype(BF16)
    a_new = (w_new[:half] - lam * w_new[half:]).astype(BF16)

    keys = per * page
    full = _dot(a_new, vn_ref[...].astype(BF16))
    for i, v in enumerate(v_part):
        full = full + _dot(a_past[:, i * keys:(i + 1) * keys], v)
    for h in range(N_HEADS):
        o_ref[h] = full[h * t_new:(h + 1) * t_new, h * V_DIM:(h + 1) * V_DIM]


def _decode_specs(seq_of, out_of, n_pages, page, t_new):
    d = D_MODEL
    rows = 2 * N_HEADS * t_new
    vec = pl.BlockSpec((1, HEAD_DIM), lambda *a: (0, 0))
    tok = pl.BlockSpec((t_new, d), lambda *a: (seq_of(*a[:-1]), 0))

    def page_spec(i, shape):
        return pl.BlockSpec((None,) + shape, lambda *a: (a[-1][seq_of(*a[:-1]), i], 0, 0))

    in_specs = [pl.BlockSpec(memory_space=pltpu.SMEM), vec, vec, vec, vec, tok, tok, tok]
    in_specs += [page_spec(i, (d, page)) for i in range(n_pages)]
    in_specs += [page_spec(i, (page * N_HEADS, V_DIM)) for i in range(n_pages)]
    out_spec = pl.BlockSpec((N_HEADS, t_new, V_DIM), lambda *a: (0, out_of(*a[:-1]), 0))
    scratch = [pltpu.VMEM((rows, page), F32), pltpu.VMEM((rows, t_new), F32)]
    return in_specs, out_spec, scratch


def _decode_operands(q, k_new, v_new, cache_kt, cache_v, rel_bias, lam_vecs, n_pages):
    return (rel_bias, *lam_vecs, q, k_new, v_new, *([cache_kt] * n_pages), *([cache_v] * n_pages))


N_CONV_IN = 9


def _conv_decode_kernel(pt_ref, *refs, tile, n_tiles, n_pages, page, t_new, lam_init):
    del pt_ref
    n_dec = N_DECODE_IN + 2 * n_pages
    conv_in, dec_in = refs[:N_CONV_IN], refs[N_CONV_IN:N_CONV_IN + n_dec]
    o_ref, st_ref, od_ref, u_ref, c_ref, bp_ref, bn_ref = refs[N_CONV_IN + n_dec:]
    first = (pl.program_id(0) == 0) & (pl.program_id(1) == 0)
    _decode_init(first, dec_in, (bp_ref, bn_ref), page=page, t_new=t_new)
    decode = functools.partial(_decode_main, dec_in, od_ref, (bp_ref, bn_ref),
                               n_pages=n_pages, page=page, t_new=t_new, lam_init=lam_init)
    _conv_p_body(pl.program_id(1), *conv_in, o_ref, st_ref, u_ref, c_ref, tile=tile, n_tiles=n_tiles,
                 alongside=decode)


def _conv_prompt_decode(h, batch, seq, conv_params, decode_operands, page_table, seq_offset,
                        lam_init, *, tile, t_new):
    d = D_MODEL
    n_tiles = seq // tile
    n_steps = batch * n_tiles
    n_pages = page_table.shape[1]
    page = decode_operands[-1].shape[1] // N_HEADS
    tok = pl.BlockSpec((tile, d), lambda b, t, pt: (b * n_tiles + t, 0))
    conv_specs = [tok, _resident((1, d)), _resident((d, 2 * d)), _resident((1, 2 * d)),
                  _resident((CONV_W, d)), _resident((1, d)), _resident((1, d)),
                  _resident((d, d)), _resident((1, d))]
    dec_specs, dec_out, dec_scratch = _decode_specs(
        lambda b, t: seq_offset + b * n_tiles + t, lambda b, t: b * n_tiles + t, n_pages, page, t_new)
    kern = functools.partial(_conv_decode_kernel, tile=tile, n_tiles=n_tiles, n_pages=n_pages,
                             page=page, t_new=t_new, lam_init=lam_init)
    grid_spec = pltpu.PrefetchScalarGridSpec(
        num_scalar_prefetch=1, grid=(batch, n_tiles),
        in_specs=conv_specs + dec_specs,
        out_specs=(tok, pl.BlockSpec((None, CONV_W - 1, d), lambda b, t, pt: (b, 0, 0)), dec_out),
        scratch_shapes=[pltpu.VMEM((d // LANES, HALO + tile, LANES), F32),
                        pltpu.VMEM((tile, d), F32)] + dec_scratch)
    return pl.pallas_call(
        kern,
        out_shape=(jax.ShapeDtypeStruct((batch * seq, d), F32),
                   jax.ShapeDtypeStruct((batch, CONV_W - 1, d), F32),
                   jax.ShapeDtypeStruct((N_HEADS, n_steps * t_new, V_DIM), F32)),
        grid_spec=grid_spec,
        compiler_params=_params(2),
        name="conv_prompt_decode",
    )(page_table, h, *conv_params, *decode_operands)


def _attn_decode_kernel(pt_ref, *refs, blk, lam_init, n_pages, page, t_new, lam_init_decode):
    del pt_ref
    n_dec = N_DECODE_IN + 2 * n_pages
    attn_in, dec_in = refs[:N_ATTN_IN], refs[N_ATTN_IN:N_ATTN_IN + n_dec]
    o_ref, od_ref, bias_ref, q2_ref, m_ref, acc_ref, bp_ref, bn_ref = refs[N_ATTN_IN + n_dec:]
    first = (pl.program_id(0) == 0) & (pl.program_id(1) == 0)
    _decode_init(first, dec_in, (bp_ref, bn_ref), page=page, t_new=t_new)
    _decode_main(dec_in, od_ref, (bp_ref, bn_ref),
                 n_pages=n_pages, page=page, t_new=t_new, lam_init=lam_init_decode)
    _attn_p_body(first, pl.program_id(1), *attn_in, o_ref, bias_ref, q2_ref, m_ref, acc_ref,
                 blk=blk, lam_init=lam_init)


def _attn_prompt_decode(qt, k, vt, rel_bias, lam_vecs, lam_init, decode_operands, page_table,
                        seq_offset, lam_init_decode, *, blk, t_new):
    batch, _, _, seq = qt.shape
    n_blk = seq // blk
    n_pages = page_table.shape[1]
    page = decode_operands[-1].shape[1] // N_HEADS
    vec = _resident((1, HEAD_DIM))
    once = dict(pipeline_mode=pl.Buffered(1))
    attn_specs = [pl.BlockSpec(memory_space=pltpu.SMEM), vec, vec, vec, vec,
                  pl.BlockSpec((None, N_HEADS, V_DIM, blk), lambda b, i, pt: (b, 0, 0, i)),
                  pl.BlockSpec((None, N_HEADS, seq, V_DIM), lambda b, i, pt: (b, 0, 0, 0), **once),
                  pl.BlockSpec((None, N_HEADS, V_DIM + SUM_ROWS, seq), lambda b, i, pt: (b, 0, 0, 0), **once)]
    dec_specs, dec_out, dec_scratch = _decode_specs(
        lambda b, i: seq_offset + b * n_blk + i, lambda b, i: b * n_blk + i, n_pages, page, t_new)
    kern = functools.partial(_attn_decode_kernel, blk=blk, lam_init=lam_init, n_pages=n_pages,
                             page=page, t_new=t_new, lam_init_decode=lam_init_decode)
    grid_spec = pltpu.PrefetchScalarGridSpec(
        num_scalar_prefetch=1, grid=(batch, n_blk),
        in_specs=attn_specs + dec_specs,
        out_specs=(pl.BlockSpec((None, N_HEADS, blk, V_DIM), lambda b, i, pt: (b, 0, i, 0)), dec_out),
        scratch_shapes=[pltpu.VMEM((2, N_HEADS, blk, blk), F32),
                        pltpu.VMEM((N_HEADS, V_DIM, 2 * blk), BF16),
                        pltpu.VMEM((N_HEADS, 1, 2 * blk), F32),
                        pltpu.VMEM((N_HEADS, V_DIM + SUM_ROWS, 2 * blk), F32)]
        + dec_scratch)
    return pl.pallas_call(
        kern,
        out_shape=(jax.ShapeDtypeStruct((batch, N_HEADS, seq, V_DIM), F32),
                   jax.ShapeDtypeStruct((N_HEADS, batch * n_blk * t_new, V_DIM), F32)),
        grid_spec=grid_spec,
        compiler_params=_params(2),
        name="attn_prompt_decode",
    )(page_table, rel_bias, *lam_vecs, qt, k, vt, *decode_operands)


def _lambda_init(layer_idx):
    return 0.8 - 0.6 * math.exp(-0.3 * layer_idx)


def kernel(x_prompt, x_sample, state_conv, cache_k, cache_v, page_table, rel_bias, conv_norm, w_pw1, b_pw1, w_dw, b_dw, conv_mid_norm, w_pw2, b_pw2, kv_norm, w_kv, k_norm, attn_norm, w_q, q_norm, lambda_q1, lambda_k1, lambda_q2, lambda_k2, sub_norm, w_o, ffn_norm, w_gate, w_up, w_down):
    batch, seq, d = x_prompt.shape
    n_b, t_new, _ = x_sample.shape
    n_a = state_conv.shape[0]
    depth = ffn_norm.shape[0]
    n_pool, page = cache_k.shape[:2]

    row = lambda a: a.reshape(1, -1)
    two = lambda a: jnp.concatenate([a, a]).reshape(1, V_DIM)
    bf = lambda a: a.astype(BF16)

    hp = x_prompt.reshape(batch * seq, d)
    ckt = jnp.transpose(cache_k, (0, 2, 3, 4, 1)).reshape(n_pool, d, page)
    cv = cache_v.reshape(n_pool, page * N_HEADS, V_DIM)
    state_t = jnp.transpose(state_conv, (0, 2, 1, 3))
    hs_t = jnp.transpose(x_sample, (1, 0, 2))

    wg_all, wu_all, wd_all = bf(w_gate), bf(w_up), bf(w_down)

    def ffn_weights(l):
        return row(ffn_norm[l]), wg_all, wu_all, wd_all, l

    def conv_weights(l):
        return (row(conv_norm[l]), bf(w_pw1[l]), row(b_pw1[l]), w_dw[l], row(b_dw[l]),
                row(conv_mid_norm[l]), bf(w_pw2[l]), row(b_pw2[l]))

    n_attn = depth - n_a
    n_pages = page_table.shape[1]
    lam_vecs = [(row(lambda_q1[j]), row(lambda_k1[j]), row(lambda_q2[j]), row(lambda_k2[j]))
                for j in range(n_attn)]
    lam_inits = [_lambda_init(n_a + j) for j in range(n_attn)]
    kv_weights = (row(kv_norm), bf(w_kv), two(k_norm))

    def q_weights(j):
        return row(attn_norm[j]), bf(w_q[j]), two(q_norm[j])

    def attn_out_weights(o, j):
        return o, row(sub_norm[j]), bf(w_o[j]), 1.0 - lam_inits[j]

    assert n_a >= 1 and n_attn == 2, "kernel schedule below is laid out for conv layers then two attention layers"
    conv_tile = attn_blk = 256
    steps = batch * (seq // conv_tile)
    assert n_a * steps == n_b and n_attn * steps == n_b, "one sample sequence's decode per prompt grid step"
    prompt = dict(tm=512, batch=batch, seq=seq)
    sample = dict(tm=512)

    conv_s = []
    for l in range(n_a):
        hs_t, ss = _conv_sample(hs_t, state_t, l, *conv_weights(l), bb=16)
        conv_s.append(ss)
        if l < n_a - 1:
            hs_t, = _layer_tail(hs_t.reshape(t_new * n_b, d), ffn_weights(l), **sample)
            hs_t = hs_t.reshape(t_new, n_b, d)
    hs = jnp.transpose(hs_t, (1, 0, 2)).reshape(n_b * t_new, d)
    hs, ksm, vsm, qs = _layer_tail(hs, ffn_weights(n_a - 1), kv=kv_weights, q=q_weights(0), **sample)
    decode = _decode_operands(qs, ksm, vsm, ckt, cv, rel_bias, lam_vecs[0], n_pages)

    conv_p, os_parts = [], []
    for l in range(n_a):
        hp, sp, od = _conv_prompt_decode(hp, batch, seq, conv_weights(l), decode, page_table,
                                         l * steps, lam_inits[0], tile=conv_tile, t_new=t_new)
        conv_p.append(sp)
        os_parts.append(od)
        if l < n_a - 1:
            hp, = _layer_tail(hp, ffn_weights(l), **prompt)
    hp, kpt, vp, kpb, vpt, qp = _layer_tail(hp, ffn_weights(n_a - 1), kv=kv_weights, q=q_weights(0),
                                            **prompt)
    os_ = jnp.concatenate(os_parts, axis=1)

    hs, qs = _layer_tail(hs, ffn_weights(n_a), attn=attn_out_weights(os_[None], 0), q=q_weights(1),
                         **sample)
    decode = _decode_operands(qs, ksm, vsm, ckt, cv, rel_bias, lam_vecs[1], n_pages)
    os_parts = []
    for j in range(n_attn):
        op, od = _attn_prompt_decode(qp, kpb, vpt, rel_bias, lam_vecs[j], lam_inits[j], decode,
                                     page_table, j * steps, lam_inits[1], blk=attn_blk, t_new=t_new)
        os_parts.append(od)
        if j + 1 < n_attn:
            hp, qp = _layer_tail(hp, ffn_weights(n_a + j), attn=attn_out_weights(op, j),
                                 q=q_weights(j + 1), **prompt)
        else:
            hp, = _layer_tail(hp, ffn_weights(n_a + j), attn=attn_out_weights(op, j), **prompt)
    hs, = _layer_tail(hs, ffn_weights(n_a + 1), attn=attn_out_weights(jnp.concatenate(os_parts, axis=1)[None], 1),
                      **sample)

    k_prompt = jnp.transpose(kpt.reshape(batch, N_HEADS, 2, HEAD_DIM, seq), (0, 4, 1, 2, 3))
    return (hp.reshape(batch, seq, d), hs.reshape(n_b, t_new, d),
            jnp.stack(conv_p), jnp.transpose(jnp.stack(conv_s), (0, 2, 1, 3)),
            k_prompt, vp.reshape(batch, seq, N_HEADS, V_DIM),
            ksm.reshape(n_b, t_new, N_HEADS, 2, HEAD_DIM), vsm.reshape(n_b, t_new, N_HEADS, V_DIM))
```
